```python
import math
import jax, jax.numpy as jnp
from jax import lax
import numpy as np


D_MODEL = 2048
BATCH = 4
SEQ = 8192
DEPTH = 1

MLA_HEADS = 8
MLA_Q_RANK = 768
MLA_KV_RANK = 512
MLA_NOPE = 128
MLA_ROPE = 64
MLA_QK = MLA_NOPE + MLA_ROPE
MLA_V = 128
MLA_WIDTH = MLA_HEADS * MLA_V
ROPE_THETA = 10000.0
Q_BLOCK = 128

RWKV_HEAD = 64
RWKV_WIDTH = D_MODEL - MLA_WIDTH
RWKV_HEADS = RWKV_WIDTH // RWKV_HEAD
DECAY_LORA = max(32, int(round(1.8 * math.sqrt(RWKV_WIDTH) / 32)) * 32)
AAA_LORA = max(32, int(round(1.8 * math.sqrt(RWKV_WIDTH) / 32)) * 32)
GATE_LORA = max(32, int(round(0.6 * RWKV_WIDTH ** 0.8 / 32)) * 32)
RWKV_SPLITS = (RWKV_WIDTH, RWKV_WIDTH, RWKV_WIDTH, DECAY_LORA, AAA_LORA, GATE_LORA)
RWKV_IN = 3 * RWKV_WIDTH + DECAY_LORA + AAA_LORA + GATE_LORA

IN_SPLITS = (MLA_Q_RANK, MLA_KV_RANK, MLA_ROPE, RWKV_IN)
IN_WIDTH = MLA_Q_RANK + MLA_KV_RANK + MLA_ROPE + RWKV_IN

FFN_HIDDEN = ((8 * D_MODEL + 3 * 256 - 1) // (3 * 256)) * 256

NORM_EPS = 1e-6
GN_EPS = 64e-5

kernel_name = 'hybrid_mla_rwkv7_block'


def _split(t, sizes):
    idx = np.cumsum(np.array(sizes))[:-1].tolist()
    return jnp.split(t, idx, axis=-1)


def rms_norm(x, g, eps=NORM_EPS):
    xf = x.astype(jnp.float32)
    y = xf * lax.rsqrt(jnp.mean(xf * xf, axis=-1, keepdims=True) + eps)
    return (y * g.astype(jnp.float32)).astype(x.dtype)


def rope_tables(S, dim):
    inv_freq = 1.0 / (ROPE_THETA ** (jnp.arange(0, dim, 2, dtype=jnp.float32) / dim))
    ang = jnp.arange(S, dtype=jnp.float32)[:, None] * inv_freq[None, :]
    return jnp.cos(ang), jnp.sin(ang)


def apply_rope(x, cos, sin):
    xf = x.astype(jnp.float32)
    x1, x2 = jnp.split(xf, 2, axis=-1)
    out = jnp.concatenate([x1 * cos - x2 * sin, x2 * cos + x1 * sin], axis=-1)
    return out.astype(x.dtype)


def mla_group(q_lat, kv_lat, k_rope, q_lat_norm, w_uq, kv_lat_norm, w_ukv,
              q_head_norm, k_nope_norm, k_rope_norm):
    B, S, _ = q_lat.shape
    q = (rms_norm(q_lat, q_lat_norm) @ w_uq).reshape(B, S, MLA_HEADS, MLA_QK)
    q = rms_norm(q, q_head_norm)
    q_nope, q_pe = q[..., :MLA_NOPE], q[..., MLA_NOPE:]
    kv = (rms_norm(kv_lat, kv_lat_norm) @ w_ukv).reshape(B, S, MLA_HEADS, MLA_NOPE + MLA_V)
    k_nope = rms_norm(kv[..., :MLA_NOPE], k_nope_norm)
    v = kv[..., MLA_NOPE:]
    k_pe = rms_norm(k_rope, k_rope_norm)
    cos, sin = rope_tables(S, MLA_ROPE)
    q_pe = apply_rope(q_pe, cos[:, None, :], sin[:, None, :])
    k_pe = apply_rope(k_pe, cos, sin)
    n_blk = S // Q_BLOCK

    def to_blocks(t):
        return t.reshape(B, n_blk, Q_BLOCK, MLA_HEADS, t.shape[-1]).swapaxes(0, 1)

    key_pos = jnp.arange(S)
    scale = MLA_QK ** -0.5

    def attend_block(args):
        qn, qp, blk = args
        s = (jnp.einsum('bqhd,bkhd->bhqk', qn, k_nope)
             + jnp.einsum('bqhd,bkd->bhqk', qp, k_pe))
        s = s.astype(jnp.float32) * scale
        q_pos = blk * Q_BLOCK + jnp.arange(Q_BLOCK)
        s = jnp.where(key_pos[None, :] <= q_pos[:, None], s, -jnp.inf)
        p = jax.nn.softmax(s, axis=-1).astype(v.dtype)
        return jnp.einsum('bhqk,bkhd->bqhd', p, v)

    o = lax.map(attend_block, (to_blocks(q_nope), to_blocks(q_pe), jnp.arange(n_blk)))
    return o.swapaxes(0, 1).reshape(B, S, MLA_WIDTH)


def rwkv7_group(p, shift_mix, w0, w2, a0, a2, g2, k_k, k_a, r_k, ln_g, ln_b):
    B, S, _ = p.shape
    f32 = jnp.float32
    shifted = jnp.pad(p[:, :-1], ((0, 0), (1, 0), (0, 0)))
    p = p + (shifted - p) * shift_mix
    r, k, v, xw, xa, xg = _split(p, RWKV_SPLITS)
    w_log = -jax.nn.softplus(-(w0 + jnp.tanh(xw) @ w2).astype(f32)) - 0.5
    decay = jnp.exp(-jnp.exp(w_log))
    a = jax.nn.sigmoid((a0 + xa @ a2).astype(f32))
    g = jax.nn.sigmoid(xg) @ g2

    def heads(t):
        return t.astype(f32).reshape(B, S, RWKV_HEADS, RWKV_HEAD)

    r, k, v, decay, a = heads(r), heads(k), heads(v), heads(decay), heads(a)
    kk = k * k_k.astype(f32).reshape(RWKV_HEADS, RWKV_HEAD)
    kk = kk * lax.rsqrt(jnp.maximum(jnp.sum(kk * kk, axis=-1, keepdims=True), 1e-24))
    k = k * (1.0 + (a - 1.0) * k_a.astype(f32).reshape(RWKV_HEADS, RWKV_HEAD))
    xs = tuple(t.swapaxes(0, 1) for t in (r, decay, k, v, -kk, kk * a))

    def step(state, inp):
        r_t, w_t, k_t, v_t, a_t, b_t = inp
        sa = jnp.einsum('bhij,bhj->bhi', state, a_t)
        state = (state * w_t[:, :, None, :]
                 + sa[..., None] * b_t[:, :, None, :]
                 + v_t[..., None] * k_t[:, :, None, :])
        return state, jnp.einsum('bhij,bhj->bhi', state, r_t)

    state0 = jnp.zeros((B, RWKV_HEADS, RWKV_HEAD, RWKV_HEAD), f32)
    _, y = lax.scan(step, state0, xs)
    y = y.swapaxes(0, 1)
    mu = jnp.mean(y, axis=-1, keepdims=True)
    var = jnp.mean(jnp.square(y - mu), axis=-1, keepdims=True)
    yn = ((y - mu) * lax.rsqrt(var + GN_EPS)).reshape(B, S, RWKV_WIDTH)
    yn = yn * ln_g.astype(f32) + ln_b.astype(f32)
    bonus = jnp.sum(r * k * r_k.astype(f32), axis=-1, keepdims=True) * v
    out = (yn + bonus.reshape(B, S, RWKV_WIDTH)) * g.astype(f32)
    return out.astype(p.dtype)


def setup_inputs(seed: int = 0) -> dict:
    key = jax.random.key(seed)
    ks = jax.random.split(key, 27)
    f32 = jnp.float32
    L = DEPTH

    def nrm(k, shape, scale):
        return jax.random.normal(k, shape, f32) * scale

    def gain(k, shape):
        return 1.0 + 0.02 * jax.random.normal(k, shape, f32)

    return {
        'x': jax.random.normal(ks[0], (BATCH, SEQ, D_MODEL), f32),
        'attn_norm_g': gain(ks[1], (L, D_MODEL)),
        'w_in': nrm(ks[2], (L, D_MODEL, IN_WIDTH), D_MODEL ** -0.5),
        'q_lat_norm': gain(ks[3], (L, MLA_Q_RANK)),
        'w_uq': nrm(ks[4], (L, MLA_Q_RANK, MLA_HEADS * MLA_QK), MLA_Q_RANK ** -0.5),
        'kv_lat_norm': gain(ks[5], (L, MLA_KV_RANK)),
        'w_ukv': nrm(ks[6], (L, MLA_KV_RANK, MLA_HEADS * (MLA_NOPE + MLA_V)), MLA_KV_RANK ** -0.5),
        'q_head_norm': gain(ks[7], (L, MLA_QK)),
        'k_nope_norm': gain(ks[8], (L, MLA_NOPE)),
        'k_rope_norm': gain(ks[9], (L, MLA_ROPE)),
        'rwkv_shift_mix': jax.random.uniform(ks[10], (L, RWKV_IN), f32),
        'rwkv_w0': jax.random.uniform(ks[11], (L, RWKV_WIDTH), f32, -6.0, -1.0),
        'rwkv_w2': nrm(ks[12], (L, DECAY_LORA, RWKV_WIDTH), 0.1 * DECAY_LORA ** -0.5),
        'rwkv_a0': nrm(ks[13], (L, RWKV_WIDTH), 0.1),
        'rwkv_a2': nrm(ks[14], (L, AAA_LORA, RWKV_WIDTH), 0.5 * AAA_LORA ** -0.5),
        'rwkv_g2': nrm(ks[15], (L, GATE_LORA, RWKV_WIDTH), GATE_LORA ** -0.5),
        'rwkv_k_k': 0.85 + 0.05 * jax.random.normal(ks[16], (L, RWKV_WIDTH), f32),
        'rwkv_k_a': 1.0 + 0.05 * jax.random.normal(ks[17], (L, RWKV_WIDTH), f32),
        'rwkv_r_k': nrm(ks[18], (L, RWKV_HEADS, RWKV_HEAD), 0.1),
        'rwkv_ln_g': gain(ks[19], (L, RWKV_WIDTH)),
        'rwkv_ln_b': nrm(ks[20], (L, RWKV_WIDTH), 0.02),
        'w_out': nrm(ks[21], (L, D_MODEL, D_MODEL), D_MODEL ** -0.5),
        'ffn_norm_g': gain(ks[22], (L, D_MODEL)),
        'w_gate': nrm(ks[23], (L, D_MODEL, FFN_HIDDEN), D_MODEL ** -0.5),
        'w_up': nrm(ks[24], (L, D_MODEL, FFN_HIDDEN), D_MODEL ** -0.5),
        'w_down': nrm(ks[25], (L, FFN_HIDDEN, D_MODEL), FFN_HIDDEN ** -0.5),
    }


def reference(x, attn_norm_g, w_in, q_lat_norm, w_uq, kv_lat_norm, w_ukv,
              q_head_norm, k_nope_norm, k_rope_norm, rwkv_shift_mix, rwkv_w0,
              rwkv_w2, rwkv_a0, rwkv_a2, rwkv_g2, rwkv_k_k, rwkv_k_a, rwkv_r_k,
              rwkv_ln_g, rwkv_ln_b, w_out, ffn_norm_g, w_gate, w_up, w_down):
    for l in range(DEPTH):
        h = rms_norm(x, attn_norm_g[l])
        proj = h @ w_in[l]
        q_lat, kv_lat, k_rope, p_rwkv = _split(proj, IN_SPLITS)
        out_a = mla_group(q_lat, kv_lat, k_rope, q_lat_norm[l], w_uq[l], kv_lat_norm[l],
                          w_ukv[l], q_head_norm[l], k_nope_norm[l], k_rope_norm[l])
        out_b = rwkv7_group(p_rwkv, rwkv_shift_mix[l], rwkv_w0[l], rwkv_w2[l], rwkv_a0[l],
                            rwkv_a2[l], rwkv_g2[l], rwkv_k_k[l], rwkv_k_a[l], rwkv_r_k[l],
                            rwkv_ln_g[l], rwkv_ln_b[l])
        x = x + jnp.concatenate([out_a, out_b], axis=-1) @ w_out[l]
        h = rms_norm(x, ffn_norm_g[l])
        x = x + (jax.nn.silu(h @ w_gate[l]) * (h @ w_up[l])) @ w_down[l]
    return x
```

```python
import functools
import math

import jax
import jax.numpy as jnp
from jax import lax
from jax.experimental import pallas as pl
from jax.experimental.pallas import tpu as pltpu

LANES = 128
HALF = LANES // 2

MLA_HEADS = 8
MLA_Q_RANK = 768
MLA_KV_RANK = 512
MLA_NOPE = 128
MLA_ROPE = 64
MLA_QK = MLA_NOPE + MLA_ROPE
MLA_V = 128
ROPE_THETA = 10000.0
RWKV_HEAD = 64
RWKV_WIDTH = 1024
RWKV_PAIRS = RWKV_WIDTH // LANES
DECAY_LORA = 64
AAA_LORA = 64
GATE_LORA = 160
NORM_EPS = 1e-6
GN_EPS = 64e-5

MLA_IN = MLA_Q_RANK + MLA_KV_RANK + LANES
SMALL_IN = 3 * LANES
CHUNK = 64

VMEM_LIMIT = 56 * 1024 * 1024

f32 = jnp.float32
bf16 = jnp.bfloat16


def _cparams(sem):
    return pltpu.CompilerParams(dimension_semantics=sem, vmem_limit_bytes=VMEM_LIMIT)


def _rms(x, g):
    return x * lax.rsqrt(jnp.mean(x * x, axis=-1, keepdims=True) + NORM_EPS) * g


def _dot(a, b):
    return jnp.dot(a.astype(bf16), b.astype(bf16), preferred_element_type=f32)


def _dot_nt(a, b):
    return lax.dot_general(a.astype(bf16), b.astype(bf16), (((1,), (1,)), ((), ())),
                           preferred_element_type=f32)


def _norm_matmul_kernel(x_ref, g_ref, w_ref, *rest, splits):
    o_refs, h_ref = rest[:-1], rest[-1]

    @pl.when(pl.program_id(1) == 0)
    def _():
        h_ref[...] = _rms(x_ref[...], g_ref[...]).astype(bf16)

    acc = jnp.dot(h_ref[...], w_ref[...], preferred_element_type=f32)
    off = 0
    for o_ref, width in zip(o_refs, splits):
        o_ref[...] = acc[:, off:off + width].astype(o_ref.dtype)
        off += width


def _norm_matmul(x, g, w, *, tm, tn, splits, out_dtypes):
    m, k = x.shape
    n = w.shape[1]
    assert m % tm == 0 and n % tn == 0 and sum(splits) == tn
    assert len(splits) == 1 or tn == n
    out_shape = [jax.ShapeDtypeStruct((m, (n // tn) * s), dt) for s, dt in zip(splits, out_dtypes)]
    return pl.pallas_call(
        functools.partial(_norm_matmul_kernel, splits=splits),
        grid=(m // tm, n // tn),
        in_specs=[pl.BlockSpec((tm, k), lambda i, j: (i, 0)),
                  pl.BlockSpec((1, k), lambda i, j: (0, 0)),
                  pl.BlockSpec((k, tn), lambda i, j: (0, j))],
        out_specs=[pl.BlockSpec((tm, s), lambda i, j: (i, j)) for s in splits],
        out_shape=out_shape,
        scratch_shapes=[pltpu.VMEM((tm, k), bf16)],
        compiler_params=_cparams(("parallel", "arbitrary")),
        name="norm_matmul",
    )(x, g, w)


def _swap_halves(x, period):
    n = x.shape[-1]
    lane = lax.broadcasted_iota(jnp.int32, x.shape, x.ndim - 1)
    first = (lane % period) < (period // 2)
    return jnp.where(first, pltpu.roll(x, n - period // 2, x.ndim - 1), pltpu.roll(x, period // 2, x.ndim - 1))


def _lanes64(slab, hi):
    if hi:
        slab = pltpu.roll(slab, HALF, 1)
    return slab[:, :HALF]


def _mla_prep_kernel(lat_ref, gq_ref, wuq_ref, gkv_ref, wukv_ref, gqn_ref, gqp_ref, gkn_ref, gkr_ref,
                     cosq_ref, sinq_ref, q_ref, k_ref, v_ref):
    lat = lat_ref[...].astype(f32)
    scale = MLA_QK ** -0.5
    rows = lat.shape[0]
    lane = lax.broadcasted_iota(jnp.int32, (rows, LANES), 1)
    lo = lane < HALF

    q = _dot(_rms(lat[:, :MLA_Q_RANK], gq_ref[...]), wuq_ref[...])
    pe = q[:, MLA_HEADS * MLA_NOPE:]
    pe_sq = pe * pe
    inv = []
    for h in range(MLA_HEADS):
        nope = q[:, h * MLA_NOPE:(h + 1) * MLA_NOPE]
        slab = pe_sq[:, (h // 2) * LANES:(h // 2 + 1) * LANES]
        ss = jnp.sum(nope * nope, axis=-1, keepdims=True)
        mine = lo if h % 2 == 0 else jnp.logical_not(lo)
        ss = ss + jnp.sum(jnp.where(mine, slab, 0.0), axis=-1, keepdims=True)
        inv.append(lax.rsqrt(ss * (1.0 / MLA_QK) + NORM_EPS))
    inv_pe = jnp.concatenate(
        [jnp.where(lo, inv[2 * p], inv[2 * p + 1]) for p in range(MLA_HEADS // 2)], axis=1)
    pe = pe * inv_pe * gqp_ref[...]
    pe = (pe * cosq_ref[...] + _swap_halves(pe, MLA_ROPE) * sinq_ref[...]) * scale
    for h in range(MLA_HEADS):
        nope = q[:, h * MLA_NOPE:(h + 1) * MLA_NOPE] * (inv[h] * scale) * gqn_ref[...]
        q_ref[0, h, :, :MLA_NOPE] = nope.astype(q_ref.dtype)
        slab = pe[:, (h // 2) * LANES:(h // 2 + 1) * LANES]
        q_ref[0, h, :, MLA_NOPE:] = _lanes64(slab, h % 2 == 1).astype(q_ref.dtype)

    kr = lat[:, MLA_Q_RANK + MLA_KV_RANK:]
    kr = kr * lax.rsqrt(jnp.sum(kr * kr, axis=-1, keepdims=True) * (1.0 / MLA_ROPE) + NORM_EPS) * gkr_ref[...]
    kr = kr * cosq_ref[:, :LANES] + _swap_halves(kr, MLA_ROPE) * sinq_ref[:, :LANES]
    kr = kr[:, :HALF].astype(k_ref.dtype)

    kv = _dot(_rms(lat[:, MLA_Q_RANK:MLA_Q_RANK + MLA_KV_RANK], gkv_ref[...]), wukv_ref[...])
    for h in range(MLA_HEADS):
        kn = kv[:, h * MLA_NOPE:(h + 1) * MLA_NOPE]
        k_ref[0, h, :, :MLA_NOPE] = _rms(kn, gkn_ref[...]).astype(k_ref.dtype)
        k_ref[0, h, :, MLA_NOPE:] = kr
        v_ref[0, h] = kv[:, (MLA_HEADS + h) * MLA_V:(MLA_HEADS + h + 1) * MLA_V].astype(v_ref.dtype)


def _mla_prep(lat, gq, wuq, gkv, wukv, gqn, gqp, gkn, gkr, cosq, sinq, *, batch, seq, tm):
    nsb = seq // tm
    full = lambda a: pl.BlockSpec(a.shape, lambda b, s: (0,) * a.ndim)
    qk_spec = pl.BlockSpec((1, MLA_HEADS, tm, MLA_QK), lambda b, s: (b, 0, s, 0))
    return pl.pallas_call(
        _mla_prep_kernel,
        grid=(batch, nsb),
        in_specs=[pl.BlockSpec((tm, MLA_IN), lambda b, s: (b * nsb + s, 0)),
                  full(gq), full(wuq), full(gkv), full(wukv), full(gqn), full(gqp), full(gkn), full(gkr),
                  pl.BlockSpec((tm, cosq.shape[1]), lambda b, s: (s, 0)),
                  pl.BlockSpec((tm, sinq.shape[1]), lambda b, s: (s, 0))],
        out_specs=[qk_spec, qk_spec,
                   pl.BlockSpec((1, MLA_HEADS, tm, MLA_V), lambda b, s: (b, 0, s, 0))],
        out_shape=[jax.ShapeDtypeStruct((batch, MLA_HEADS, seq, MLA_QK), bf16),
                   jax.ShapeDtypeStruct((batch, MLA_HEADS, seq, MLA_QK), bf16),
                   jax.ShapeDtypeStruct((batch, MLA_HEADS, seq, MLA_V), bf16)],
        compiler_params=_cparams(("parallel", "parallel")),
        name="mla_prep",
    )(lat, gq, wuq, gkv, wukv, gqn, gqp, gkn, gkr, cosq, sinq)


def _flash_kernel(q_ref, k_ref, v_ref, o_ref, *, tq, tk):
    qi = pl.program_id(2)
    q = q_ref[0, 0]

    def step(j, carry, masked):
        m, l, acc = carry
        start = pl.multiple_of(j * tk, tk)
        k = k_ref[0, 0, pl.ds(start, tk), :]
        v = v_ref[0, 0, pl.ds(start, tk), :]
        s = lax.dot_general(q, k, (((1,), (1,)), ((), ())), preferred_element_type=f32)
        if masked:
            q_pos = qi * tq + lax.broadcasted_iota(jnp.int32, s.shape, 0)
            k_pos = j * tk + lax.broadcasted_iota(jnp.int32, s.shape, 1)
            s = jnp.where(k_pos <= q_pos, s, -jnp.inf)
        m_new = jnp.maximum(m, jnp.max(s, axis=-1, keepdims=True))
        p = jnp.exp(s - m_new)
        alpha = jnp.exp(m - m_new)
        l = alpha * l + jnp.sum(p, axis=-1, keepdims=True)
        acc = alpha * acc + jnp.dot(p.astype(v.dtype), v, preferred_element_type=f32)
        return m_new, l, acc

    init = (jnp.full((tq, 1), -jnp.inf, f32), jnp.zeros((tq, 1), f32), jnp.zeros((tq, MLA_V), f32))
    n_full = qi * (tq // tk)
    carry = lax.fori_loop(0, n_full, functools.partial(step, masked=False), init)
    for d in range(tq // tk):
        carry = step(n_full + d, carry, masked=True)
    _, l, acc = carry
    o_ref[0] = (acc / l).astype(o_ref.dtype)


def _flash_attention(q, k, v, *, tq, tk):
    batch, heads, seq, _ = q.shape
    assert seq % tq == 0 and tq % tk == 0
    return pl.pallas_call(
        functools.partial(_flash_kernel, tq=tq, tk=tk),
        grid=(batch, heads, seq // tq),
        in_specs=[pl.BlockSpec((1, 1, tq, MLA_QK), lambda b, h, i: (b, h, i, 0)),
                  pl.BlockSpec((1, 1, seq, MLA_QK), lambda b, h, i: (b, h, 0, 0)),
                  pl.BlockSpec((1, 1, seq, MLA_V), lambda b, h, i: (b, h, 0, 0))],
        out_specs=pl.BlockSpec((1, tq, MLA_V), lambda b, h, i: (b, i, h)),
        out_shape=jax.ShapeDtypeStruct((batch, seq, heads * MLA_V), bf16),
        compiler_params=_cparams(("parallel", "parallel", "arbitrary")),
        name="flash_attention",
    )(q, k, v)


def _pair_sum(x, lo):
    s_lo = jnp.sum(jnp.where(lo, x, 0.0), axis=-1, keepdims=True)
    s_hi = jnp.sum(jnp.where(lo, 0.0, x), axis=-1, keepdims=True)
    return jnp.where(lo, s_lo, s_hi)


def _stack2(x, lo):
    return jnp.concatenate([jnp.where(lo, x, 0.0), jnp.where(lo, 0.0, x)], axis=0)


def _shift_rows(x, carry_ref):
    row = lax.broadcasted_iota(jnp.int32, x.shape, 0)
    prev = jnp.where(row == 0, carry_ref[7:8, :], pltpu.roll(x, 1, 0))
    carry_ref[...] = x[x.shape[0] - 8:, :]
    return prev


def _rwkv_kernel(r_ref, k_ref, v_ref, sm_ref, mixr_ref, mixk_ref, mixv_ref, mixs_ref,
                 w0_ref, w2_ref, a0_ref, a2_ref, g2_ref, kk_ref, ka_ref, rk_ref, lng_ref, lnb_ref,
                 o_ref, h_ref, cr_ref, ck_ref, cv_ref, cs_ref, *, chunk):
    @pl.when(pl.program_id(2) == 0)
    def _():
        h_ref[...] = jnp.zeros_like(h_ref)
        cr_ref[...] = jnp.zeros_like(cr_ref)
        ck_ref[...] = jnp.zeros_like(ck_ref)
        cv_ref[...] = jnp.zeros_like(cv_ref)
        cs_ref[...] = jnp.zeros_like(cs_ref)

    rows = r_ref.shape[1]
    L = chunk
    lane = lax.broadcasted_iota(jnp.int32, (rows, LANES), 1)
    lo = lane < HALF

    def mixed(x, carry_ref, mix_ref):
        return x + (_shift_rows(x, carry_ref) - x) * mix_ref[...]

    r = mixed(r_ref[0], cr_ref, mixr_ref)
    k = mixed(k_ref[0], ck_ref, mixk_ref)
    v = mixed(v_ref[0], cv_ref, mixv_ref)
    sm = mixed(sm_ref[0], cs_ref, mixs_ref)

    z = w0_ref[...] + _dot(jnp.tanh(sm[:, :LANES]), w2_ref[...])
    softplus = jnp.maximum(-z, 0.0) + jnp.log(1.0 + jnp.exp(-jnp.abs(z)))
    ld = -jnp.exp(-softplus - 0.5)
    a_sig = 1.0 / (1.0 + jnp.exp(-(a0_ref[...] + _dot(sm[:, :LANES], a2_ref[...]))))
    gate = _dot(1.0 / (1.0 + jnp.exp(-sm[:, LANES:])), g2_ref[...])

    kk = k * kk_ref[...]
    kk = kk * lax.rsqrt(jnp.maximum(_pair_sum(kk * kk, lo), 1e-24))
    k = k * (1.0 + (a_sig - 1.0) * ka_ref[...])
    a_in = -kk
    b_in = kk * a_sig
    bonus = _pair_sum(r * k * rk_ref[...], lo) * v

    t_i = lax.broadcasted_iota(jnp.int32, (L, 2 * L), 0)
    s_i = lax.broadcasted_iota(jnp.int32, (L, 2 * L), 1) % L
    strict = s_i < t_i
    incl = s_i <= t_i
    tri = (lax.broadcasted_iota(jnp.int32, (L, L), 1) <= lax.broadcasted_iota(jnp.int32, (L, L), 0)).astype(bf16)
    rr = lax.broadcasted_iota(jnp.int32, (2 * L, 2 * L), 0)
    cc = lax.broadcasted_iota(jnp.int32, (2 * L, 2 * L), 1)
    bd = (rr < L) == (cc < L)
    eye = rr == cc
    lo_l = lax.broadcasted_iota(jnp.int32, (L, LANES), 1) < HALF
    last_row = lax.broadcasted_iota(jnp.int32, (L, LANES), 0) == L - 1

    def blockdiag(pair):
        return jnp.where(bd, jnp.concatenate([pair, pair], axis=0), 0.0)

    h = h_ref[...]
    ys = []
    for c in range(rows // L):
        sl = slice(c * L, (c + 1) * L)
        rc, kc, vc, ac, bc, ldc = r[sl], k[sl], v[sl], a_in[sl], b_in[sl], ld[sl]
        d1 = ldc.astype(bf16)
        d2 = (ldc - d1.astype(f32)).astype(bf16)
        d3 = (ldc - d1.astype(f32) - d2.astype(f32)).astype(bf16)
        cs = (jnp.dot(tri, d1, preferred_element_type=f32) + jnp.dot(tri, d2, preferred_element_type=f32)
              + jnp.dot(tri, d3, preferred_element_type=f32))
        last = jnp.sum(jnp.where(last_row, cs, 0.0), axis=0, keepdims=True)
        g_inv = jnp.exp(-cs)
        g_rel = jnp.exp(last - cs)
        at = ac * jnp.exp(cs - ldc)
        rt = rc * jnp.exp(cs)
        bt = bc * g_inv
        kt = kc * g_inv
        rhs = jnp.concatenate([_stack2(bt, lo_l), _stack2(kt, lo_l)], axis=0)
        gram = _dot_nt(jnp.concatenate([at, rt], axis=0), rhs)
        a_ab = jnp.where(strict, gram[:L, :2 * L], 0.0)
        a_ak = jnp.where(strict, gram[:L, 2 * L:], 0.0)
        a_rb = jnp.where(incl, gram[L:, :2 * L], 0.0)
        a_rk = jnp.where(incl, gram[L:, 2 * L:], 0.0)
        n = a_ab
        t = jnp.where(eye[:L], 1.0, 0.0) + jnp.where(eye[L:], 1.0, 0.0) + n
        span = 1
        while 2 * span < L:
            n = _dot(n, blockdiag(n))
            t = t + _dot(t, blockdiag(n))
            span *= 2
        v2 = _stack2(vc, lo_l)
        av = _dot(a_ak, v2)
        pq = _dot(t, jnp.concatenate([_stack2(at, lo_l), _stack2(av, lo_l)], axis=1))
        p, q = pq[:, :LANES], pq[:, LANES:]
        p2, q2 = _stack2(p, lo_l), _stack2(q, lo_l)
        zeros = jnp.zeros_like(v2)
        yy = _dot(jnp.concatenate([a_rb, a_rk], axis=1),
                  jnp.concatenate([jnp.concatenate([p2, q2], axis=1),
                                   jnp.concatenate([zeros, v2], axis=1)], axis=0))
        y1 = rt + yy[:, :LANES]
        y0 = yy[:, LANES:]
        bk_t = jnp.concatenate([bc * g_rel, kc * g_rel], axis=0).T
        mn = _dot(bk_t, jnp.concatenate([jnp.concatenate([p, q], axis=1),
                                         jnp.concatenate([jnp.zeros_like(vc), vc], axis=1)], axis=0))
        m = jnp.where(eye, jnp.exp(last), 0.0) + jnp.where(bd, mn[:, :LANES], 0.0)
        nn = jnp.where(bd, mn[:, LANES:], 0.0)
        yh = _dot(jnp.concatenate([y1, m], axis=0), h)
        ys.append(yh[:L] + y0)
        h = yh[L:] + nn
    h_ref[...] = h
    y = jnp.concatenate(ys, axis=0)

    mu = _pair_sum(y, lo) * (1.0 / RWKV_HEAD)
    yc = y - mu
    var = _pair_sum(yc * yc, lo) * (1.0 / RWKV_HEAD)
    yn = yc * lax.rsqrt(var + GN_EPS) * lng_ref[...] + lnb_ref[...]
    o_ref[0] = ((yn + bonus) * gate).astype(o_ref.dtype)


def _rwkv7(rkv, small, mix_rkv, mix_small, w0, w2p, a0, a2p, g2p, k_k, k_a, r_k, ln_g, ln_b, *, tm, chunk=CHUNK):
    batch, seq, _ = rkv.shape
    assert seq % tm == 0 and tm % chunk == 0
    col = lambda off: pl.BlockSpec((1, tm, LANES), lambda b, p, s: (b, s, off + p))
    vec = lambda off=0: pl.BlockSpec((1, LANES), lambda b, p, s: (0, off + p))
    mat = lambda rows: pl.BlockSpec((rows, LANES), lambda b, p, s: (0, p))
    carry = pltpu.VMEM((8, LANES), f32)
    return pl.pallas_call(
        functools.partial(_rwkv_kernel, chunk=chunk),
        grid=(batch, RWKV_PAIRS, seq // tm),
        in_specs=[col(0), col(RWKV_PAIRS), col(2 * RWKV_PAIRS),
                  pl.BlockSpec((1, tm, SMALL_IN), lambda b, p, s: (b, s, 0)),
                  vec(0), vec(RWKV_PAIRS), vec(2 * RWKV_PAIRS),
                  pl.BlockSpec((1, SMALL_IN), lambda b, p, s: (0, 0)),
                  vec(), mat(LANES), vec(), mat(LANES), mat(2 * LANES),
                  vec(), vec(), vec(), vec(), vec()],
        out_specs=pl.BlockSpec((1, tm, LANES), lambda b, p, s: (b, s, p)),
        out_shape=jax.ShapeDtypeStruct((batch, seq, RWKV_WIDTH), bf16),
        scratch_shapes=[pltpu.VMEM((LANES, LANES), f32), carry, carry, carry, pltpu.VMEM((8, SMALL_IN), f32)],
        compiler_params=_cparams(("parallel", "parallel", "arbitrary")),
        name="rwkv7",
    )(rkv, rkv, rkv, small, mix_rkv, mix_rkv, mix_rkv, mix_small,
      w0, w2p, a0, a2p, g2p, k_k, k_a, r_k, ln_g, ln_b)


def _out_proj_kernel(x_ref, a_ref, b_ref, wa_ref, wb_ref, o_ref):
    acc = jnp.dot(a_ref[...], wa_ref[...], preferred_element_type=f32)
    acc = acc + jnp.dot(b_ref[...], wb_ref[...], preferred_element_type=f32)
    o_ref[...] = x_ref[...] + acc


def _out_proj(x, a, b, wa, wb, *, tm, tn):
    m, n = x.shape
    return pl.pallas_call(
        _out_proj_kernel,
        grid=(m // tm, n // tn),
        in_specs=[pl.BlockSpec((tm, tn), lambda i, j: (i, j)),
                  pl.BlockSpec((tm, a.shape[1]), lambda i, j: (i, 0)),
                  pl.BlockSpec((tm, b.shape[1]), lambda i, j: (i, 0)),
                  pl.BlockSpec((wa.shape[0], tn), lambda i, j: (0, j)),
                  pl.BlockSpec((wb.shape[0], tn), lambda i, j: (0, j))],
        out_specs=pl.BlockSpec((tm, tn), lambda i, j: (i, j)),
        out_shape=jax.ShapeDtypeStruct((m, n), f32),
        compiler_params=_cparams(("parallel", "parallel")),
        name="out_proj",
    )(x, a, b, wa, wb)


def _ffn_kernel(x_ref, g_ref, wg_ref, wu_ref, wd_ref, o_ref, h_ref, acc_ref):
    j = pl.program_id(1)

    @pl.when(j == 0)
    def _():
        h_ref[...] = _rms(x_ref[...], g_ref[...]).astype(bf16)
        acc_ref[...] = jnp.zeros_like(acc_ref)

    h = h_ref[...]
    gate = jnp.dot(h, wg_ref[...], preferred_element_type=f32)
    up = jnp.dot(h, wu_ref[...], preferred_element_type=f32)
    act = (gate / (1.0 + jnp.exp(-gate))) * up
    acc_ref[...] += jnp.dot(act.astype(bf16), wd_ref[...], preferred_element_type=f32)

    @pl.when(j == pl.num_programs(1) - 1)
    def _():
        o_ref[...] = x_ref[...] + acc_ref[...]


def _ffn(x, g, wg, wu, wd, *, tm, tf):
    m, d = x.shape
    hidden = wg.shape[1]
    assert m % tm == 0 and hidden % tf == 0
    return pl.pallas_call(
        _ffn_kernel,
        grid=(m // tm, hidden // tf),
        in_specs=[pl.BlockSpec((tm, d), lambda i, j: (i, 0)),
                  pl.BlockSpec((1, d), lambda i, j: (0, 0)),
                  pl.BlockSpec((d, tf), lambda i, j: (0, j)),
                  pl.BlockSpec((d, tf), lambda i, j: (0, j)),
                  pl.BlockSpec((tf, d), lambda i, j: (j, 0))],
        out_specs=pl.BlockSpec((tm, d), lambda i, j: (i, 0)),
        out_shape=jax.ShapeDtypeStruct((m, d), f32),
        scratch_shapes=[pltpu.VMEM((tm, d), bf16), pltpu.VMEM((tm, d), f32)],
        compiler_params=_cparams(("parallel", "arbitrary")),
        name="ffn",
    )(x, g, wg, wu, wd)


def _pad_cols(a, width):
    return jnp.pad(a, ((0, 0), (0, width - a.shape[1])))


def _pad_rows(a, rows, before=0):
    return jnp.pad(a, ((before, rows - before - a.shape[0]), (0, 0)))


def _rope_tables(seq):
    inv_freq = 1.0 / (ROPE_THETA ** (jnp.arange(0, MLA_ROPE, 2, dtype=f32) / MLA_ROPE))
    ang = jnp.arange(seq, dtype=f32)[:, None] * inv_freq[None, :]
    cos, sin = jnp.cos(ang), jnp.sin(ang)
    cos_t = jnp.tile(jnp.concatenate([cos, cos], axis=1), (1, MLA_HEADS))
    sin_t = jnp.tile(jnp.concatenate([-sin, sin], axis=1), (1, MLA_HEADS))
    return cos_t, sin_t


def _layer(x, attn_norm_g, w_in, q_lat_norm, w_uq, kv_lat_norm, w_ukv, q_head_norm, k_nope_norm, k_rope_norm,
           shift_mix, w0, w2, a0, a2, g2, k_k, k_a, r_k, ln_g, ln_b, w_out, ffn_norm_g, w_gate, w_up, w_down,
           cos_t, sin_t, *, tiles):
    batch, seq, d_model = x.shape
    tokens = batch * seq
    row = lambda a: a.reshape(1, -1)
    x2d = x.reshape(tokens, d_model)

    c_q, c_kv, c_kr = MLA_Q_RANK, MLA_Q_RANK + MLA_KV_RANK, MLA_Q_RANK + MLA_KV_RANK + MLA_ROPE
    c_v = c_kr + 3 * RWKV_WIDTH
    w_lat = jnp.concatenate([_pad_cols(w_in[:, :c_kr], MLA_IN), _pad_cols(w_in[:, c_v:], SMALL_IN)], axis=1).astype(bf16)
    w_rkv = w_in[:, c_kr:c_v].astype(bf16)
    mix_rkv = row(shift_mix[:3 * RWKV_WIDTH])
    mix_small = _pad_cols(row(shift_mix[3 * RWKV_WIDTH:]), SMALL_IN)

    lat, small = _norm_matmul(x2d, row(attn_norm_g), w_lat, tm=tiles["in_tm"], tn=MLA_IN + SMALL_IN,
                              splits=(MLA_IN, SMALL_IN), out_dtypes=(bf16, f32))
    (rkv,) = _norm_matmul(x2d, row(attn_norm_g), w_rkv, tm=tiles["in_tm"], tn=tiles["in_tn"],
                          splits=(tiles["in_tn"],), out_dtypes=(f32,))

    wuq = w_uq.reshape(MLA_Q_RANK, MLA_HEADS, MLA_QK)
    wuq = jnp.concatenate([wuq[:, :, :MLA_NOPE].reshape(MLA_Q_RANK, -1),
                           wuq[:, :, MLA_NOPE:].reshape(MLA_Q_RANK, -1)], axis=1).astype(bf16)
    wukv = w_ukv.reshape(MLA_KV_RANK, MLA_HEADS, MLA_NOPE + MLA_V)
    wukv = jnp.concatenate([wukv[:, :, :MLA_NOPE].reshape(MLA_KV_RANK, -1),
                            wukv[:, :, MLA_NOPE:].reshape(MLA_KV_RANK, -1)], axis=1).astype(bf16)
    gqn = row(q_head_norm[:MLA_NOPE])
    gqp = row(jnp.tile(q_head_norm[MLA_NOPE:], MLA_HEADS))
    gkr = _pad_cols(row(k_rope_norm), LANES)
    q, k, v = _mla_prep(lat, row(q_lat_norm), wuq, row(kv_lat_norm), wukv, gqn, gqp, row(k_nope_norm), gkr,
                        cos_t, sin_t, batch=batch, seq=seq, tm=tiles["mla_tm"])
    out_a = _flash_attention(q, k, v, tq=tiles["tq"], tk=tiles["tk"])

    w2p = _pad_rows(w2, LANES).astype(bf16)
    a2p = _pad_rows(a2, LANES, before=DECAY_LORA).astype(bf16)
    g2p = _pad_rows(g2, 2 * LANES).astype(bf16)
    out_b = _rwkv7(rkv.reshape(batch, seq, 3 * RWKV_WIDTH), small.reshape(batch, seq, SMALL_IN),
                   mix_rkv, mix_small, row(w0), w2p, row(a0), a2p, g2p, row(k_k), row(k_a), row(r_k),
                   row(ln_g), row(ln_b), tm=tiles["rw_tm"])

    mla_width = MLA_HEADS * MLA_V
    x2 = _out_proj(x2d, out_a.reshape(tokens, mla_width), out_b.reshape(tokens, RWKV_WIDTH),
                   w_out[:mla_width].astype(bf16), w_out[mla_width:].astype(bf16),
                   tm=tiles["out_tm"], tn=tiles["out_tn"])
    y = _ffn(x2, row(ffn_norm_g), w_gate.astype(bf16), w_up.astype(bf16), w_down.astype(bf16),
             tm=tiles["ffn_tm"], tf=tiles["ffn_tf"])
    return y.reshape(batch, seq, d_model)


def _tiles(seq):
    t = lambda want: math.gcd(want, seq)
    return dict(in_tm=t(512), in_tn=1024, mla_tm=t(256), tq=t(512), tk=t(512), rw_tm=t(256),
                out_tm=t(512), out_tn=1024, ffn_tm=t(512), ffn_tf=512)


def kernel(x, attn_norm_g, w_in, q_lat_norm, w_uq, kv_lat_norm, w_ukv, q_head_norm, k_nope_norm, k_rope_norm, rwkv_shift_mix, rwkv_w0, rwkv_w2, rwkv_a0, rwkv_a2, rwkv_g2, rwkv_k_k, rwkv_k_a, rwkv_r_k, rwkv_ln_g, rwkv_ln_b, w_out, ffn_norm_g, w_gate, w_up, w_down):
    seq = x.shape[1]
    cos_t, sin_t = _rope_tables(seq)
    tiles = _tiles(seq)
    params = (attn_norm_g, w_in, q_lat_norm, w_uq, kv_lat_norm, w_ukv, q_head_norm, k_nope_norm, k_rope_norm,
              rwkv_shift_mix, rwkv_w0, rwkv_w2, rwkv_a0, rwkv_a2, rwkv_g2, rwkv_k_k, rwkv_k_a,
              rwkv_r_k.reshape(rwkv_r_k.shape[0], -1), rwkv_ln_g, rwkv_ln_b, w_out, ffn_norm_g, w_gate, w_up, w_down)
    for l in range(attn_norm_g.shape[0]):
        x = _layer(x, *(p[l] for p in params), cos_t, sin_t, tiles=tiles)
    return x
```

```python
import functools
import math

import jax
import jax.numpy as jnp
from jax import lax
from jax.experimental import pallas as pl
from jax.experimental.pallas import tpu as pltpu

LANES = 128
HALF = LANES // 2

MLA_HEADS = 8
MLA_Q_RANK = 768
MLA_KV_RANK = 512
MLA_NOPE = 128
MLA_ROPE = 64
MLA_QK = MLA_NOPE + MLA_ROPE
MLA_V = 128
ROPE_THETA = 10000.0
RWKV_HEAD = 64
RWKV_WIDTH = 1024
RWKV_PAIRS = RWKV_WIDTH // LANES
DECAY_LORA = 64
AAA_LORA = 64
GATE_LORA = 160
NORM_EPS = 1e-6
GN_EPS = 64e-5

MLA_IN = MLA_Q_RANK + MLA_KV_RANK + LANES
SMALL_IN = 3 * LANES
CHUNK = 64

VMEM_LIMIT = 56 * 1024 * 1024

f32 = jnp.float32
bf16 = jnp.bfloat16


def _cparams(sem):
    return pltpu.CompilerParams(dimension_semantics=sem, vmem_limit_bytes=VMEM_LIMIT)


def _rms(x, g):
    return x * lax.rsqrt(jnp.mean(x * x, axis=-1, keepdims=True) + NORM_EPS) * g


def _dot(a, b):
    return jnp.dot(a.astype(bf16), b.astype(bf16), preferred_element_type=f32)


def _dot_nt(a, b):
    return lax.dot_general(a.astype(bf16), b.astype(bf16), (((1,), (1,)), ((), ())),
                           preferred_element_type=f32)


def _norm_matmul_kernel(x_ref, g_ref, w_ref, *rest, splits):
    o_refs, h_ref = rest[:-1], rest[-1]

    @pl.when(pl.program_id(1) == 0)
    def _():
        h_ref[...] = _rms(x_ref[...], g_ref[...]).astype(bf16)

    acc = jnp.dot(h_ref[...], w_ref[...], preferred_element_type=f32)
    off = 0
    for o_ref, width in zip(o_refs, splits):
        o_ref[...] = acc[:, off:off + width].astype(o_ref.dtype)
        off += width


def _norm_matmul(x, g, w, *, tm, tn, splits, out_dtypes):
    m, k = x.shape
    n = w.shape[1]
    assert m % tm == 0 and n % tn == 0 and sum(splits) == tn
    assert len(splits) == 1 or tn == n
    out_shape = [jax.ShapeDtypeStruct((m, (n // tn) * s), dt) for s, dt in zip(splits, out_dtypes)]
    return pl.pallas_call(
        functools.partial(_norm_matmul_kernel, splits=splits),
        grid=(m // tm, n // tn),
        in_specs=[pl.BlockSpec((tm, k), lambda i, j: (i, 0)),
                  pl.BlockSpec((1, k), lambda i, j: (0, 0)),
                  pl.BlockSpec((k, tn), lambda i, j: (0, j))],
        out_specs=[pl.BlockSpec((tm, s), lambda i, j: (i, j)) for s in splits],
        out_shape=out_shape,
        scratch_shapes=[pltpu.VMEM((tm, k), bf16)],
        compiler_params=_cparams(("parallel", "arbitrary")),
        name="norm_matmul",
    )(x, g, w)


def _swap_halves(x, period):
    n = x.shape[-1]
    lane = lax.broadcasted_iota(jnp.int32, x.shape, x.ndim - 1)
    first = (lane % period) < (period // 2)
    return jnp.where(first, pltpu.roll(x, n - period // 2, x.ndim - 1), pltpu.roll(x, period // 2, x.ndim - 1))


def _lanes64(slab, hi):
    if hi:
        slab = pltpu.roll(slab, HALF, 1)
    return slab[:, :HALF]


def _mla_prep_kernel(lat_ref, gq_ref, wuq_ref, gkv_ref, wukv_ref, gqn_ref, gqp_ref, gkn_ref, gkr_ref,
                     cosq_ref, sinq_ref, q_ref, k_ref, v_ref):
    lat = lat_ref[...].astype(f32)
    scale = MLA_QK ** -0.5
    rows = lat.shape[0]
    lane = lax.broadcasted_iota(jnp.int32, (rows, LANES), 1)
    lo = lane < HALF

    q = _dot(_rms(lat[:, :MLA_Q_RANK], gq_ref[...]), wuq_ref[...])
    pe = q[:, MLA_HEADS * MLA_NOPE:]
    pe_sq = pe * pe
    inv = []
    for h in range(MLA_HEADS):
        nope = q[:, h * MLA_NOPE:(h + 1) * MLA_NOPE]
        slab = pe_sq[:, (h // 2) * LANES:(h // 2 + 1) * LANES]
        ss = jnp.sum(nope * nope, axis=-1, keepdims=True)
        mine = lo if h % 2 == 0 else jnp.logical_not(lo)
        ss = ss + jnp.sum(jnp.where(mine, slab, 0.0), axis=-1, keepdims=True)
        inv.append(lax.rsqrt(ss * (1.0 / MLA_QK) + NORM_EPS))
    inv_pe = jnp.concatenate(
        [jnp.where(lo, inv[2 * p], inv[2 * p + 1]) for p in range(MLA_HEADS // 2)], axis=1)
    pe = pe * inv_pe * gqp_ref[...]
    pe = (pe * cosq_ref[...] + _swap_halves(pe, MLA_ROPE) * sinq_ref[...]) * scale
    for h in range(MLA_HEADS):
        nope = q[:, h * MLA_NOPE:(h + 1) * MLA_NOPE] * (inv[h] * scale) * gqn_ref[...]
        q_ref[0, h, :, :MLA_NOPE] = nope.astype(q_ref.dtype)
        slab = pe[:, (h // 2) * LANES:(h // 2 + 1) * LANES]
        q_ref[0, h, :, MLA_NOPE:] = _lanes64(slab, h % 2 == 1).astype(q_ref.dtype)

    kr = lat[:, MLA_Q_RANK + MLA_KV_RANK:]
    kr = kr * lax.rsqrt(jnp.sum(kr * kr, axis=-1, keepdims=True) * (1.0 / MLA_ROPE) + NORM_EPS) * gkr_ref[...]
    kr = kr * cosq_ref[:, :LANES] + _swap_halves(kr, MLA_ROPE) * sinq_ref[:, :LANES]
    kr = kr[:, :HALF].astype(k_ref.dtype)

    kv = _dot(_rms(lat[:, MLA_Q_RANK:MLA_Q_RANK + MLA_KV_RANK], gkv_ref[...]), wukv_ref[...])
    for h in range(MLA_HEADS):
        kn = kv[:, h * MLA_NOPE:(h + 1) * MLA_NOPE]
        k_ref[0, h, :, :MLA_NOPE] = _rms(kn, gkn_ref[...]).astype(k_ref.dtype)
        k_ref[0, h, :, MLA_NOPE:] = kr
        v_ref[0, h] = kv[:, (MLA_HEADS + h) * MLA_V:(MLA_HEADS + h + 1) * MLA_V].astype(v_ref.dtype)


def _mla_prep(lat, gq, wuq, gkv, wukv, gqn, gqp, gkn, gkr, cosq, sinq, *, batch, seq, tm):
    nsb = seq // tm
    full = lambda a: pl.BlockSpec(a.shape, lambda b, s: (0,) * a.ndim)
    qk_spec = pl.BlockSpec((1, MLA_HEADS, tm, MLA_QK), lambda b, s: (b, 0, s, 0))
    return pl.pallas_call(
        _mla_prep_kernel,
        grid=(batch, nsb),
        in_specs=[pl.BlockSpec((tm, MLA_IN), lambda b, s: (b * nsb + s, 0)),
                  full(gq), full(wuq), full(gkv), full(wukv), full(gqn), full(gqp), full(gkn), full(gkr),
                  pl.BlockSpec((tm, cosq.shape[1]), lambda b, s: (s, 0)),
                  pl.BlockSpec((tm, sinq.shape[1]), lambda b, s: (s, 0))],
        out_specs=[qk_spec, qk_spec,
                   pl.BlockSpec((1, MLA_HEADS, tm, MLA_V), lambda b, s: (b, 0, s, 0))],
        out_shape=[jax.ShapeDtypeStruct((batch, MLA_HEADS, seq, MLA_QK), bf16),
                   jax.ShapeDtypeStruct((batch, MLA_HEADS, seq, MLA_QK), bf16),
                   jax.ShapeDtypeStruct((batch, MLA_HEADS, seq, MLA_V), bf16)],
        compiler_params=_cparams(("parallel", "parallel")),
        name="mla_prep",
    )(lat, gq, wuq, gkv, wukv, gqn, gqp, gkn, gkr, cosq, sinq)


def _flash_kernel(q_ref, k_ref, v_ref, o_ref, *, tq, tk):
    qi = pl.program_id(2)
    q = q_ref[0, 0]

    def step(j, carry, masked):
        m, l, acc = carry
        start = pl.multiple_of(j * tk, tk)
        k = k_ref[0, 0, pl.ds(start, tk), :]
        v = v_ref[0, 0, pl.ds(start, tk), :]
        s = lax.dot_general(q, k, (((1,), (1,)), ((), ())), preferred_element_type=f32)
        if masked:
            q_pos = qi * tq + lax.broadcasted_iota(jnp.int32, s.shape, 0)
            k_pos = j * tk + lax.broadcasted_iota(jnp.int32, s.shape, 1)
            s = jnp.where(k_pos <= q_pos, s, -jnp.inf)
        m_new = jnp.maximum(m, jnp.max(s, axis=-1, keepdims=True))
        p = jnp.exp(s - m_new)
        alpha = jnp.exp(m - m_new)
        l = alpha * l + jnp.sum(p, axis=-1, keepdims=True)
        acc = alpha * acc + jnp.dot(p.astype(v.dtype), v, preferred_element_type=f32)
        return m_new, l, acc

    init = (jnp.full((tq, 1), -jnp.inf, f32), jnp.zeros((tq, 1), f32), jnp.zeros((tq, MLA_V), f32))
    n_full = qi * (tq // tk)
    carry = lax.fori_loop(0, n_full, functools.partial(step, masked=False), init)
    for d in range(tq // tk):
        carry = step(n_full + d, carry, masked=True)
    _, l, acc = carry
    o_ref[0] = (acc / l).astype(o_ref.dtype)


def _flash_attention(q, k, v, *, tq, tk):
    batch, heads, seq, _ = q.shape
    assert seq % tq == 0 and tq % tk == 0
    return pl.pallas_call(
        functools.partial(_flash_kernel, tq=tq, tk=tk),
        grid=(batch, heads, seq // tq),
        in_specs=[pl.BlockSpec((1, 1, tq, MLA_QK), lambda b, h, i: (b, h, i, 0)),
                  pl.BlockSpec((1, 1, seq, MLA_QK), lambda b, h, i: (b, h, 0, 0)),
                  pl.BlockSpec((1, 1, seq, MLA_V), lambda b, h, i: (b, h, 0, 0))],
        out_specs=pl.BlockSpec((1, tq, MLA_V), lambda b, h, i: (b, i, h)),
        out_shape=jax.ShapeDtypeStruct((batch, seq, heads * MLA_V), bf16),
        compiler_params=_cparams(("parallel", "parallel", "arbitrary")),
        name="flash_attention",
    )(q, k, v)


def _slabs(x):
    return [x[:, p * LANES:(p + 1) * LANES] for p in range(x.shape[1] // LANES)]


def _pair_sum(x):
    lo = lax.broadcasted_iota(jnp.int32, (x.shape[0], LANES), 1) < HALF
    out = []
    for s in _slabs(x):
        s_lo = jnp.sum(jnp.where(lo, s, 0.0), axis=-1, keepdims=True)
        s_hi = jnp.sum(jnp.where(lo, 0.0, s), axis=-1, keepdims=True)
        out.append(jnp.where(lo, s_lo, s_hi))
    return jnp.concatenate(out, axis=1)


def _stack2(x, lo):
    return jnp.concatenate([jnp.where(lo, x, 0.0), jnp.where(lo, 0.0, x)], axis=0)


def _shift_rows(x, carry_ref):
    row = lax.broadcasted_iota(jnp.int32, x.shape, 0)
    prev = jnp.where(row == 0, carry_ref[7:8, :], pltpu.roll(x, 1, 0))
    carry_ref[...] = x[x.shape[0] - 8:, :]
    return prev


def _rwkv_kernel(r_ref, k_ref, v_ref, sm_ref, mixr_ref, mixk_ref, mixv_ref, mixs_ref,
                 w0_ref, w2_ref, a0_ref, a2_ref, g2_ref, kk_ref, ka_ref, rk_ref, lng_ref, lnb_ref,
                 o_ref, h_ref, cr_ref, ck_ref, cv_ref, cs_ref, *, chunk):
    @pl.when(pl.program_id(1) == 0)
    def _():
        h_ref[...] = jnp.zeros_like(h_ref)
        cr_ref[...] = jnp.zeros_like(cr_ref)
        ck_ref[...] = jnp.zeros_like(ck_ref)
        cv_ref[...] = jnp.zeros_like(cv_ref)
        cs_ref[...] = jnp.zeros_like(cs_ref)

    rows = r_ref.shape[1]
    L = chunk

    def mixed(x, carry_ref, mix_ref):
        return x + (_shift_rows(x, carry_ref) - x) * mix_ref[...]

    r = mixed(r_ref[0], cr_ref, mixr_ref)
    k = mixed(k_ref[0], ck_ref, mixk_ref)
    v = mixed(v_ref[0], cv_ref, mixv_ref)
    sm = mixed(sm_ref[0], cs_ref, mixs_ref)

    z = w0_ref[...] + _dot(jnp.tanh(sm[:, :LANES]), w2_ref[...])
    softplus = jnp.maximum(-z, 0.0) + jnp.log(1.0 + jnp.exp(-jnp.abs(z)))
    ld = -jnp.exp(-softplus - 0.5)
    a_sig = 1.0 / (1.0 + jnp.exp(-(a0_ref[...] + _dot(sm[:, :LANES], a2_ref[...]))))
    gate = _dot(1.0 / (1.0 + jnp.exp(-sm[:, LANES:])), g2_ref[...])

    kk = k * kk_ref[...]
    kk = kk * lax.rsqrt(jnp.maximum(_pair_sum(kk * kk), 1e-24))
    k = k * (1.0 + (a_sig - 1.0) * ka_ref[...])
    a_in = -kk
    b_in = kk * a_sig
    bonus = _pair_sum(r * k * rk_ref[...]) * v

    t_i = lax.broadcasted_iota(jnp.int32, (L, 2 * L), 0)
    s_i = lax.broadcasted_iota(jnp.int32, (L, 2 * L), 1) % L
    strict = s_i < t_i
    incl = s_i <= t_i
    tri = (lax.broadcasted_iota(jnp.int32, (L, L), 1) <= lax.broadcasted_iota(jnp.int32, (L, L), 0)).astype(bf16)
    rr = lax.broadcasted_iota(jnp.int32, (2 * L, 2 * L), 0)
    cc = lax.broadcasted_iota(jnp.int32, (2 * L, 2 * L), 1)
    bd = (rr < L) == (cc < L)
    eye = rr == cc
    eye_pair = jnp.where(eye[:L], 1.0, 0.0) + jnp.where(eye[L:], 1.0, 0.0)
    lo = lax.broadcasted_iota(jnp.int32, (L, LANES), 1) < HALF
    last_row = lax.broadcasted_iota(jnp.int32, (L, r.shape[1]), 0) == L - 1

    def blockdiag(pair):
        return jnp.where(bd, jnp.concatenate([pair, pair], axis=0), 0.0)

    n_chunks = rows // L
    n_pairs = r.shape[1] // LANES
    at, rt, bt, kt, bh, kh, vv, g_last = [], [], [], [], [], [], [], []
    for c in range(n_chunks):
        sl = slice(c * L, (c + 1) * L)
        ldc = ld[sl]
        d1 = ldc.astype(bf16)
        d2 = (ldc - d1.astype(f32)).astype(bf16)
        d3 = (ldc - d1.astype(f32) - d2.astype(f32)).astype(bf16)
        cs = (jnp.dot(tri, d1, preferred_element_type=f32) + jnp.dot(tri, d2, preferred_element_type=f32)
              + jnp.dot(tri, d3, preferred_element_type=f32))
        last = jnp.sum(jnp.where(last_row, cs, 0.0), axis=0, keepdims=True)
        g_inv = jnp.exp(-cs)
        g_rel = jnp.exp(last - cs)
        at += _slabs(a_in[sl] * jnp.exp(cs - ldc))
        rt += _slabs(r[sl] * jnp.exp(cs))
        bt += _slabs(b_in[sl] * g_inv)
        kt += _slabs(k[sl] * g_inv)
        bh += _slabs(b_in[sl] * g_rel)
        kh += _slabs(k[sl] * g_rel)
        vv += _slabs(v[sl])
        g_last += _slabs(jnp.exp(last))
    nu = range(n_chunks * n_pairs)
    gram = [_dot_nt(jnp.concatenate([at[u], rt[u]], axis=0),
                    jnp.concatenate([_stack2(bt[u], lo), _stack2(kt[u], lo)], axis=0)) for u in nu]
    a_ab = [jnp.where(strict, gram[u][:L, :2 * L], 0.0) for u in nu]
    a_ak = [jnp.where(strict, gram[u][:L, 2 * L:], 0.0) for u in nu]
    a_rb = [jnp.where(incl, gram[u][L:, :2 * L], 0.0) for u in nu]
    a_rk = [jnp.where(incl, gram[u][L:, 2 * L:], 0.0) for u in nu]
    v2 = [_stack2(vv[u], lo) for u in nu]
    av = [_dot(a_ak[u], v2[u]) for u in nu]
    n = a_ab
    t = [eye_pair + n[u] for u in nu]
    span = 1
    while 2 * span < L:
        n = [_dot(n[u], blockdiag(n[u])) for u in nu]
        t = [t[u] + _dot(t[u], blockdiag(n[u])) for u in nu]
        span *= 2
    pq = [_dot(t[u], jnp.concatenate([_stack2(at[u], lo), _stack2(av[u], lo)], axis=1)) for u in nu]
    zeros2 = jnp.zeros((2 * L, LANES), f32)
    zeros1 = jnp.zeros((L, LANES), f32)
    yy, mn = [], []
    for u in nu:
        p, q = pq[u][:, :LANES], pq[u][:, LANES:]
        yy.append(_dot(jnp.concatenate([a_rb[u], a_rk[u]], axis=1),
                       jnp.concatenate([jnp.concatenate([_stack2(p, lo), _stack2(q, lo)], axis=1),
                                        jnp.concatenate([zeros2, v2[u]], axis=1)], axis=0)))
        bk_t = jnp.concatenate([bh[u], kh[u]], axis=0).T
        mn.append(_dot(bk_t, jnp.concatenate([pq[u], jnp.concatenate([zeros1, vv[u]], axis=1)], axis=0)))
    hs = [h_ref[p] for p in range(n_pairs)]
    y_chunks = []
    for c in range(n_chunks):
        ys = []
        for p in range(n_pairs):
            u = c * n_pairs + p
            y1 = rt[u] + yy[u][:, :LANES]
            m = jnp.where(eye, g_last[u], 0.0) + jnp.where(bd, mn[u][:, :LANES], 0.0)
            yh = _dot(jnp.concatenate([y1, m], axis=0), hs[p])
            ys.append(yh[:L] + yy[u][:, LANES:])
            hs[p] = yh[L:] + jnp.where(bd, mn[u][:, LANES:], 0.0)
        y_chunks.append(jnp.concatenate(ys, axis=1))
    for p in range(n_pairs):
        h_ref[p] = hs[p]
    y = jnp.concatenate(y_chunks, axis=0)

    mu = _pair_sum(y) * (1.0 / RWKV_HEAD)
    yc = y - mu
    var = _pair_sum(yc * yc) * (1.0 / RWKV_HEAD)
    yn = yc * lax.rsqrt(var + GN_EPS) * lng_ref[...] + lnb_ref[...]
    o_ref[0] = ((yn + bonus) * gate).astype(o_ref.dtype)


def _rwkv7(rkv, small, mix_rkv, mix_small, w0, w2p, a0, a2p, g2p, k_k, k_a, r_k, ln_g, ln_b, *, tm, chunk=CHUNK):
    batch, seq, _ = rkv.shape
    assert seq % tm == 0 and tm % chunk == 0
    width = RWKV_WIDTH
    col = lambda j: pl.BlockSpec((1, tm, width), lambda b, s: (b, s, j))
    vec = lambda j=0: pl.BlockSpec((1, width), lambda b, s: (0, j))
    full = lambda a: pl.BlockSpec(a.shape, lambda b, s: (0,) * a.ndim)
    carry = pltpu.VMEM((8, width), f32)
    return pl.pallas_call(
        functools.partial(_rwkv_kernel, chunk=chunk),
        grid=(batch, seq // tm),
        in_specs=[col(0), col(1), col(2),
                  pl.BlockSpec((1, tm, SMALL_IN), lambda b, s: (b, s, 0)),
                  vec(0), vec(1), vec(2), full(mix_small),
                  vec(), full(w2p), vec(), full(a2p), full(g2p),
                  vec(), vec(), vec(), vec(), vec()],
        out_specs=pl.BlockSpec((1, tm, width), lambda b, s: (b, s, 0)),
        out_shape=jax.ShapeDtypeStruct((batch, seq, width), bf16),
        scratch_shapes=[pltpu.VMEM((RWKV_PAIRS, LANES, LANES), f32), carry, carry, carry,
                        pltpu.VMEM((8, SMALL_IN), f32)],
        compiler_params=_cparams(("parallel", "arbitrary")),
        name="rwkv7",
    )(rkv, rkv, rkv, small, mix_rkv, mix_rkv, mix_rkv, mix_small,
      w0, w2p, a0, a2p, g2p, k_k, k_a, r_k, ln_g, ln_b)


def _out_proj_kernel(x_ref, a_ref, b_ref, wa_ref, wb_ref, o_ref):
    acc = jnp.dot(a_ref[...], wa_ref[...], preferred_element_type=f32)
    acc = acc + jnp.dot(b_ref[...], wb_ref[...], preferred_element_type=f32)
    o_ref[...] = x_ref[...] + acc


def _out_proj(x, a, b, wa, wb, *, tm, tn):
    m, n = x.shape
    return pl.pallas_call(
        _out_proj_kernel,
        grid=(m // tm, n // tn),
        in_specs=[pl.BlockSpec((tm, tn), lambda i, j: (i, j)),
                  pl.BlockSpec((tm, a.shape[1]), lambda i, j: (i, 0)),
                  pl.BlockSpec((tm, b.shape[1]), lambda i, j: (i, 0)),
                  pl.BlockSpec((wa.shape[0], tn), lambda i, j: (0, j)),
                  pl.BlockSpec((wb.shape[0], tn), lambda i, j: (0, j))],
        out_specs=pl.BlockSpec((tm, tn), lambda i, j: (i, j)),
        out_shape=jax.ShapeDtypeStruct((m, n), f32),
        compiler_params=_cparams(("parallel", "parallel")),
        name="out_proj",
    )(x, a, b, wa, wb)


def _ffn_kernel(x_ref, g_ref, wg_ref, wu_ref, wd_ref, o_ref, h_ref, acc_ref):
    j = pl.program_id(1)

    @pl.when(j == 0)
    def _():
        h_ref[...] = _rms(x_ref[...], g_ref[...]).astype(bf16)
        acc_ref[...] = jnp.zeros_like(acc_ref)

    h = h_ref[...]
    gate = jnp.dot(h, wg_ref[...], preferred_element_type=f32)
    up = jnp.dot(h, wu_ref[...], preferred_element_type=f32)
    act = (gate / (1.0 + jnp.exp(-gate))) * up
    acc_ref[...] += jnp.dot(act.astype(bf16), wd_ref[...], preferred_element_type=f32)

    @pl.when(j == pl.num_programs(1) - 1)
    def _():
        o_ref[...] = x_ref[...] + acc_ref[...]


def _ffn(x, g, wg, wu, wd, *, tm, tf):
    m, d = x.shape
    hidden = wg.shape[1]
    assert m % tm == 0 and hidden % tf == 0
    return pl.pallas_call(
        _ffn_kernel,
        grid=(m // tm, hidden // tf),
        in_specs=[pl.BlockSpec((tm, d), lambda i, j: (i, 0)),
                  pl.BlockSpec((1, d), lambda i, j: (0, 0)),
                  pl.BlockSpec((d, tf), lambda i, j: (0, j)),
                  pl.BlockSpec((d, tf), lambda i, j: (0, j)),
                  pl.BlockSpec((tf, d), lambda i, j: (j, 0))],
        out_specs=pl.BlockSpec((tm, d), lambda i, j: (i, 0)),
        out_shape=jax.ShapeDtypeStruct((m, d), f32),
        scratch_shapes=[pltpu.VMEM((tm, d), bf16), pltpu.VMEM((tm, d), f32)],
        compiler_params=_cparams(("parallel", "arbitrary")),
        name="ffn",
    )(x, g, wg, wu, wd)


def _pad_cols(a, width):
    return jnp.pad(a, ((0, 0), (0, width - a.shape[1])))


def _pad_rows(a, rows, before=0):
    return jnp.pad(a, ((before, rows - before - a.shape[0]), (0, 0)))


def _rope_tables(seq):
    inv_freq = 1.0 / (ROPE_THETA ** (jnp.arange(0, MLA_ROPE, 2, dtype=f32) / MLA_ROPE))
    ang = jnp.arange(seq, dtype=f32)[:, None] * inv_freq[None, :]
    cos, sin = jnp.cos(ang), jnp.sin(ang)
    cos_t = jnp.tile(jnp.concatenate([cos, cos], axis=1), (1, MLA_HEADS))
    sin_t = jnp.tile(jnp.concatenate([-sin, sin], axis=1), (1, MLA_HEADS))
    return cos_t, sin_t


def _layer(x, attn_norm_g, w_in, q_lat_norm, w_uq, kv_lat_norm, w_ukv, q_head_norm, k_nope_norm, k_rope_norm,
           shift_mix, w0, w2, a0, a2, g2, k_k, k_a, r_k, ln_g, ln_b, w_out, ffn_norm_g, w_gate, w_up, w_down,
           cos_t, sin_t, *, tiles):
    batch, seq, d_model = x.shape
    tokens = batch * seq
    row = lambda a: a.reshape(1, -1)
    x2d = x.reshape(tokens, d_model)

    c_kr = MLA_Q_RANK + MLA_KV_RANK + MLA_ROPE
    c_v = c_kr + 3 * RWKV_WIDTH
    w_lat = jnp.concatenate([_pad_cols(w_in[:, :c_kr], MLA_IN), _pad_cols(w_in[:, c_v:], SMALL_IN)], axis=1).astype(bf16)
    w_rkv = w_in[:, c_kr:c_v].astype(bf16)
    mix_rkv = row(shift_mix[:3 * RWKV_WIDTH])
    mix_small = _pad_cols(row(shift_mix[3 * RWKV_WIDTH:]), SMALL_IN)

    lat, small = _norm_matmul(x2d, row(attn_norm_g), w_lat, tm=tiles["in_tm"], tn=MLA_IN + SMALL_IN,
                              splits=(MLA_IN, SMALL_IN), out_dtypes=(bf16, f32))
    (rkv,) = _norm_matmul(x2d, row(attn_norm_g), w_rkv, tm=tiles["in_tm"], tn=tiles["in_tn"],
                          splits=(tiles["in_tn"],), out_dtypes=(f32,))

    wuq = w_uq.reshape(MLA_Q_RANK, MLA_HEADS, MLA_QK)
    wuq = jnp.concatenate([wuq[:, :, :MLA_NOPE].reshape(MLA_Q_RANK, -1),
                           wuq[:, :, MLA_NOPE:].reshape(MLA_Q_RANK, -1)], axis=1).astype(bf16)
    wukv = w_ukv.reshape(MLA_KV_RANK, MLA_HEADS, MLA_NOPE + MLA_V)
    wukv = jnp.concatenate([wukv[:, :, :MLA_NOPE].reshape(MLA_KV_RANK, -1),
                            wukv[:, :, MLA_NOPE:].reshape(MLA_KV_RANK, -1)], axis=1).astype(bf16)
    gqn = row(q_head_norm[:MLA_NOPE])
    gqp = row(jnp.tile(q_head_norm[MLA_NOPE:], MLA_HEADS))
    gkr = _pad_cols(row(k_rope_norm), LANES)
    q, k, v = _mla_prep(lat, row(q_lat_norm), wuq, row(kv_lat_norm), wukv, gqn, gqp, row(k_nope_norm), gkr,
                        cos_t, sin_t, batch=batch, seq=seq, tm=tiles["mla_tm"])
    out_a = _flash_attention(q, k, v, tq=tiles["tq"], tk=tiles["tk"])

    w2p = _pad_rows(w2, LANES).astype(bf16)
    a2p = _pad_rows(a2, LANES, before=DECAY_LORA).astype(bf16)
    g2p = _pad_rows(g2, 2 * LANES).astype(bf16)
    out_b = _rwkv7(rkv.reshape(batch, seq, 3 * RWKV_WIDTH), small.reshape(batch, seq, SMALL_IN),
                   mix_rkv, mix_small, row(w0), w2p, row(a0), a2p, g2p, row(k_k), row(k_a), row(r_k),
                   row(ln_g), row(ln_b), tm=tiles["rw_tm"])

    mla_width = MLA_HEADS * MLA_V
    x2 = _out_proj(x2d, out_a.reshape(tokens, mla_width), out_b.reshape(tokens, RWKV_WIDTH),
                   w_out[:mla_width].astype(bf16), w_out[mla_width:].astype(bf16),
                   tm=tiles["out_tm"], tn=tiles["out_tn"])
    y = _ffn(x2, row(ffn_norm_g), w_gate.astype(bf16), w_up.astype(bf16), w_down.astype(bf16),
             tm=tiles["ffn_tm"], tf=tiles["ffn_tf"])
    return y.reshape(batch, seq, d_model)


def _tiles(seq):
    t = lambda want: math.gcd(want, seq)
    return dict(in_tm=t(512), in_tn=1024, mla_tm=t(256), tq=t(512), tk=t(512), rw_tm=t(128),
                out_tm=t(512), out_tn=1024, ffn_tm=t(512), ffn_tf=512)


def kernel(x, attn_norm_g, w_in, q_lat_norm, w_uq, kv_lat_norm, w_ukv, q_head_norm, k_nope_norm, k_rope_norm, rwkv_shift_mix, rwkv_w0, rwkv_w2, rwkv_a0, rwkv_a2, rwkv_g2, rwkv_k_k, rwkv_k_a, rwkv_r_k, rwkv_ln_g, rwkv_ln_b, w_out, ffn_norm_g, w_gate, w_up, w_down):
    seq = x.shape[1]
    cos_t, sin_t = _rope_tables(seq)
    tiles = _tiles(seq)
    params = (attn_norm_g, w_in, q_lat_norm, w_uq, kv_lat_norm, w_ukv, q_head_norm, k_nope_norm, k_rope_norm,
              rwkv_shift_mix, rwkv_w0, rwkv_w2, rwkv_a0, rwkv_a2, rwkv_g2, rwkv_k_k, rwkv_k_a,
              rwkv_r_k.reshape(rwkv_r_k.shape[0], -1), rwkv_ln_g, rwkv_ln_b, w_out, ffn_norm_g, w_gate, w_up, w_down)
    for l in range(attn_norm_g.shape[0]):
        x = _layer(x, *(p[l] for p in params), cos_t, sin_t, tiles=tiles)
    return x
```

```python
import functools
import math

import jax
import jax.numpy as jnp
from jax import lax
from jax.experimental import pallas as pl
from jax.experimental.pallas import tpu as pltpu

LANES = 128
HALF = LANES // 2

MLA_HEADS = 8
MLA_Q_RANK = 768
MLA_KV_RANK = 512
MLA_NOPE = 128
MLA_ROPE = 64
MLA_QK = MLA_NOPE + MLA_ROPE
MLA_V = 128
ROPE_THETA = 10000.0
RWKV_HEAD = 64
RWKV_WIDTH = 1024
RWKV_PAIRS = RWKV_WIDTH // LANES
DECAY_LORA = 64
AAA_LORA = 64
GATE_LORA = 160
NORM_EPS = 1e-6
GN_EPS = 64e-5

MLA_IN = MLA_Q_RANK + MLA_KV_RANK + LANES
SMALL_IN = 3 * LANES
CHUNK = 64

VMEM_LIMIT = 56 * 1024 * 1024

f32 = jnp.float32
bf16 = jnp.bfloat16


def _cparams(sem):
    return pltpu.CompilerParams(dimension_semantics=sem, vmem_limit_bytes=VMEM_LIMIT)


def _rms(x, g):
    return x * lax.rsqrt(jnp.mean(x * x, axis=-1, keepdims=True) + NORM_EPS) * g


def _dot(a, b):
    return jnp.dot(a.astype(bf16), b.astype(bf16), preferred_element_type=f32)


def _dot_nt(a, b):
    return lax.dot_general(a.astype(bf16), b.astype(bf16), (((1,), (1,)), ((), ())),
                           preferred_element_type=f32)


def _norm_matmul_kernel(x_ref, g_ref, w_ref, *rest, splits):
    o_refs, h_ref = rest[:-1], rest[-1]

    @pl.when(pl.program_id(1) == 0)
    def _():
        h_ref[...] = _rms(x_ref[...], g_ref[...]).astype(bf16)

    acc = jnp.dot(h_ref[...], w_ref[...], preferred_element_type=f32)
    off = 0
    for o_ref, width in zip(o_refs, splits):
        o_ref[...] = acc[:, off:off + width].astype(o_ref.dtype)
        off += width


def _norm_matmul(x, g, w, *, tm, tn, splits, out_dtypes):
    m, k = x.shape
    n = w.shape[1]
    assert m % tm == 0 and n % tn == 0 and sum(splits) == tn
    assert len(splits) == 1 or tn == n
    out_shape = [jax.ShapeDtypeStruct((m, (n // tn) * s), dt) for s, dt in zip(splits, out_dtypes)]
    return pl.pallas_call(
        functools.partial(_norm_matmul_kernel, splits=splits),
        grid=(m // tm, n // tn),
        in_specs=[pl.BlockSpec((tm, k), lambda i, j: (i, 0)),
                  pl.BlockSpec((1, k), lambda i, j: (0, 0)),
                  pl.BlockSpec((k, tn), lambda i, j: (0, j))],
        out_specs=[pl.BlockSpec((tm, s), lambda i, j: (i, j)) for s in splits],
        out_shape=out_shape,
        scratch_shapes=[pltpu.VMEM((tm, k), bf16)],
        compiler_params=_cparams(("parallel", "arbitrary")),
        name="norm_matmul",
    )(x, g, w)


def _swap_halves(x, period):
    n = x.shape[-1]
    lane = lax.broadcasted_iota(jnp.int32, x.shape, x.ndim - 1)
    first = (lane % period) < (period // 2)
    return jnp.where(first, pltpu.roll(x, n - period // 2, x.ndim - 1), pltpu.roll(x, period // 2, x.ndim - 1))


def _lanes64(slab, hi):
    if hi:
        slab = pltpu.roll(slab, HALF, 1)
    return slab[:, :HALF]


def _mla_prep_kernel(lat_ref, gq_ref, wuq_ref, gkv_ref, wukv_ref, gqn_ref, gqp_ref, gkn_ref, gkr_ref,
                     cosq_ref, sinq_ref, q_ref, k_ref, v_ref):
    lat = lat_ref[...].astype(f32)
    scale = MLA_QK ** -0.5 * math.log2(math.e)
    rows = lat.shape[0]
    lane = lax.broadcasted_iota(jnp.int32, (rows, LANES), 1)
    lo = lane < HALF

    q = _dot(_rms(lat[:, :MLA_Q_RANK], gq_ref[...]), wuq_ref[...])
    pe = q[:, MLA_HEADS * MLA_NOPE:]
    pe_sq = pe * pe
    inv = []
    for h in range(MLA_HEADS):
        nope = q[:, h * MLA_NOPE:(h + 1) * MLA_NOPE]
        slab = pe_sq[:, (h // 2) * LANES:(h // 2 + 1) * LANES]
        ss = jnp.sum(nope * nope, axis=-1, keepdims=True)
        mine = lo if h % 2 == 0 else jnp.logical_not(lo)
        ss = ss + jnp.sum(jnp.where(mine, slab, 0.0), axis=-1, keepdims=True)
        inv.append(lax.rsqrt(ss * (1.0 / MLA_QK) + NORM_EPS))
    inv_pe = jnp.concatenate(
        [jnp.where(lo, inv[2 * p], inv[2 * p + 1]) for p in range(MLA_HEADS // 2)], axis=1)
    pe = pe * inv_pe * gqp_ref[...]
    pe = (pe * cosq_ref[...] + _swap_halves(pe, MLA_ROPE) * sinq_ref[...]) * scale
    for h in range(MLA_HEADS):
        nope = q[:, h * MLA_NOPE:(h + 1) * MLA_NOPE] * (inv[h] * scale) * gqn_ref[...]
        q_ref[0, h, :, :MLA_NOPE] = nope.astype(q_ref.dtype)
        slab = pe[:, (h // 2) * LANES:(h // 2 + 1) * LANES]
        q_ref[0, h, :, MLA_NOPE:] = _lanes64(slab, h % 2 == 1).astype(q_ref.dtype)

    kr = lat[:, MLA_Q_RANK + MLA_KV_RANK:]
    kr = kr * lax.rsqrt(jnp.sum(kr * kr, axis=-1, keepdims=True) * (1.0 / MLA_ROPE) + NORM_EPS) * gkr_ref[...]
    kr = kr * cosq_ref[:, :LANES] + _swap_halves(kr, MLA_ROPE) * sinq_ref[:, :LANES]
    kr = kr[:, :HALF].astype(k_ref.dtype)

    kv = _dot(_rms(lat[:, MLA_Q_RANK:MLA_Q_RANK + MLA_KV_RANK], gkv_ref[...]), wukv_ref[...])
    for h in range(MLA_HEADS):
        kn = kv[:, h * MLA_NOPE:(h + 1) * MLA_NOPE]
        k_ref[0, h, :, :MLA_NOPE] = _rms(kn, gkn_ref[...]).astype(k_ref.dtype)
        k_ref[0, h, :, MLA_NOPE:] = kr
        vh = kv[:, (MLA_HEADS + h) * MLA_V:(MLA_HEADS + h + 1) * MLA_V]
        v_ref[0, h, 0] = vh.T.astype(v_ref.dtype)


def _mla_prep(lat, gq, wuq, gkv, wukv, gqn, gqp, gkn, gkr, cosq, sinq, *, batch, seq, tm):
    nsb = seq // tm
    full = lambda a: pl.BlockSpec(a.shape, lambda b, s: (0,) * a.ndim)
    qk_spec = pl.BlockSpec((1, MLA_HEADS, tm, MLA_QK), lambda b, s: (b, 0, s, 0))
    return pl.pallas_call(
        _mla_prep_kernel,
        grid=(batch, nsb),
        in_specs=[pl.BlockSpec((tm, MLA_IN), lambda b, s: (b * nsb + s, 0)),
                  full(gq), full(wuq), full(gkv), full(wukv), full(gqn), full(gqp), full(gkn), full(gkr),
                  pl.BlockSpec((tm, cosq.shape[1]), lambda b, s: (s, 0)),
                  pl.BlockSpec((tm, sinq.shape[1]), lambda b, s: (s, 0))],
        out_specs=[qk_spec, qk_spec,
                   pl.BlockSpec((1, MLA_HEADS, 1, MLA_V, tm), lambda b, s: (b, 0, s, 0, 0))],
        out_shape=[jax.ShapeDtypeStruct((batch, MLA_HEADS, seq, MLA_QK), bf16),
                   jax.ShapeDtypeStruct((batch, MLA_HEADS, seq, MLA_QK), bf16),
                   jax.ShapeDtypeStruct((batch, MLA_HEADS, nsb, MLA_V, tm), bf16)],
        compiler_params=_cparams(("parallel", "parallel")),
        name="mla_prep",
    )(lat, gq, wuq, gkv, wukv, gqn, gqp, gkn, gkr, cosq, sinq)


def _flash_kernel(q_ref, k_ref, vt_ref, o_ref, *, tq, tk):
    assert tq == 2 * tk
    qi = pl.program_id(2)
    halves = (q_ref[0, 0, :tk, :], q_ref[0, 0, tk:, :])

    def scores(j, q):
        k = k_ref[0, 0, pl.ds(pl.multiple_of(j * tk, tk), tk), :]
        return lax.dot_general(k, q, (((1,), (1,)), ((), ())), preferred_element_type=f32)

    def update(j, s, carry, diagonal):
        m, l, acc = carry
        if diagonal:
            s = jnp.where(lax.broadcasted_iota(jnp.int32, s.shape, 0) <= lax.broadcasted_iota(jnp.int32, s.shape, 1),
                          s, -jnp.inf)
        m_new = jnp.maximum(m, jnp.max(s, axis=0, keepdims=True))
        p = jnp.exp2(s - m_new)
        alpha = jnp.exp2(m - m_new)
        l = alpha * l + jnp.sum(p, axis=0, keepdims=True)
        acc = alpha * acc + jnp.dot(vt_ref[0, 0, j], p.astype(bf16), preferred_element_type=f32)
        return m_new, l, acc

    def body(i, carry):
        first, second = carry
        j0, j1 = 2 * i, 2 * i + 1
        s_a0 = scores(j0, halves[0])
        s_b0 = scores(j0, halves[1])
        s_a1 = scores(j1, halves[0])
        first = update(j0, s_a0, first, diagonal=False)
        s_b1 = scores(j1, halves[1])
        second = update(j0, s_b0, second, diagonal=False)
        first = update(j1, s_a1, first, diagonal=False)
        second = update(j1, s_b1, second, diagonal=False)
        return first, second

    init = (jnp.full((1, tk), -jnp.inf, f32), jnp.zeros((1, tk), f32), jnp.zeros((MLA_V, tk), f32))
    first, second = lax.fori_loop(0, qi, body, (init, init))
    s_first, s_second, s_last = scores(2 * qi, halves[0]), scores(2 * qi, halves[1]), scores(2 * qi + 1, halves[1])
    first = update(2 * qi, s_first, first, diagonal=True)
    second = update(2 * qi, s_second, second, diagonal=False)
    second = update(2 * qi + 1, s_last, second, diagonal=True)
    for half, (_, l, acc) in enumerate((first, second)):
        o_ref[0, half * tk:(half + 1) * tk, :] = (acc / l).T.astype(o_ref.dtype)


def _flash_attention(q, k, vt, *, tq, tk):
    batch, heads, seq, _ = q.shape
    assert seq % tq == 0 and tq == 2 * tk and vt.shape == (batch, heads, seq // tk, MLA_V, tk)
    return pl.pallas_call(
        functools.partial(_flash_kernel, tq=tq, tk=tk),
        grid=(batch, heads, seq // tq),
        in_specs=[pl.BlockSpec((1, 1, tq, MLA_QK), lambda b, h, i: (b, h, i, 0)),
                  pl.BlockSpec((1, 1, seq, MLA_QK), lambda b, h, i: (b, h, 0, 0)),
                  pl.BlockSpec((1, 1, seq // tk, MLA_V, tk), lambda b, h, i: (b, h, 0, 0, 0))],
        out_specs=pl.BlockSpec((1, tq, MLA_V), lambda b, h, i: (b, i, h)),
        out_shape=jax.ShapeDtypeStruct((batch, seq, heads * MLA_V), bf16),
        compiler_params=_cparams(("parallel", "parallel", "arbitrary")),
        name="flash_attention",
    )(q, k, vt)


def _slabs(x):
    return [x[:, p * LANES:(p + 1) * LANES] for p in range(x.shape[1] // LANES)]


def _pair_sum(x):
    lo = lax.broadcasted_iota(jnp.int32, (x.shape[0], LANES), 1) < HALF
    out = []
    for s in _slabs(x):
        s_lo = jnp.sum(jnp.where(lo, s, 0.0), axis=-1, keepdims=True)
        s_hi = jnp.sum(jnp.where(lo, 0.0, s), axis=-1, keepdims=True)
        out.append(jnp.where(lo, s_lo, s_hi))
    return jnp.concatenate(out, axis=1)


def _stack2(x, lo):
    return jnp.concatenate([jnp.where(lo, x, 0.0), jnp.where(lo, 0.0, x)], axis=0)


def _shift_rows(x, carry_ref):
    row = lax.broadcasted_iota(jnp.int32, x.shape, 0)
    prev = jnp.where(row == 0, carry_ref[7:8, :], pltpu.roll(x, 1, 0))
    carry_ref[...] = x[x.shape[0] - 8:, :]
    return prev


def _rwkv_kernel(r_ref, k_ref, v_ref, sm_ref, mixr_ref, mixk_ref, mixv_ref, mixs_ref,
                 w0_ref, w2_ref, a0_ref, a2_ref, g2_ref, kk_ref, ka_ref, rk_ref, lng_ref, lnb_ref,
                 o_ref, h_ref, cr_ref, ck_ref, cv_ref, cs_ref, *, chunk):
    @pl.when(pl.program_id(1) == 0)
    def _():
        h_ref[...] = jnp.zeros_like(h_ref)
        cr_ref[...] = jnp.zeros_like(cr_ref)
        ck_ref[...] = jnp.zeros_like(ck_ref)
        cv_ref[...] = jnp.zeros_like(cv_ref)
        cs_ref[...] = jnp.zeros_like(cs_ref)

    rows = r_ref.shape[1]
    L = chunk

    def mixed(x, carry_ref, mix_ref):
        return x + (_shift_rows(x, carry_ref) - x) * mix_ref[...]

    r = mixed(r_ref[0], cr_ref, mixr_ref)
    k = mixed(k_ref[0], ck_ref, mixk_ref)
    v = mixed(v_ref[0], cv_ref, mixv_ref)
    sm = mixed(sm_ref[0], cs_ref, mixs_ref)

    z = w0_ref[...] + _dot(jnp.tanh(sm[:, :LANES]), w2_ref[...])
    softplus = jnp.maximum(-z, 0.0) + jnp.log(1.0 + jnp.exp(-jnp.abs(z)))
    ld = -jnp.exp(-softplus - 0.5)
    a_sig = 1.0 / (1.0 + jnp.exp(-(a0_ref[...] + _dot(sm[:, :LANES], a2_ref[...]))))
    gate = _dot(1.0 / (1.0 + jnp.exp(-sm[:, LANES:])), g2_ref[...])

    kk = k * kk_ref[...]
    kk = kk * lax.rsqrt(jnp.maximum(_pair_sum(kk * kk), 1e-24))
    k = k * (1.0 + (a_sig - 1.0) * ka_ref[...])
    a_in = -kk
    b_in = kk * a_sig
    bonus = _pair_sum(r * k * rk_ref[...]) * v

    t_i = lax.broadcasted_iota(jnp.int32, (L, 2 * L), 0)
    s_i = lax.broadcasted_iota(jnp.int32, (L, 2 * L), 1) % L
    strict = s_i < t_i
    incl = s_i <= t_i
    tri = (lax.broadcasted_iota(jnp.int32, (L, L), 1) <= lax.broadcasted_iota(jnp.int32, (L, L), 0)).astype(bf16)
    rr = lax.broadcasted_iota(jnp.int32, (2 * L, 2 * L), 0)
    cc = lax.broadcasted_iota(jnp.int32, (2 * L, 2 * L), 1)
    bd = (rr < L) == (cc < L)
    eye = rr == cc
    eye_pair = jnp.where(eye[:L], 1.0, 0.0) + jnp.where(eye[L:], 1.0, 0.0)
    lo = lax.broadcasted_iota(jnp.int32, (L, LANES), 1) < HALF
    last_row = lax.broadcasted_iota(jnp.int32, (L, r.shape[1]), 0) == L - 1

    def blockdiag(pair):
        return jnp.where(bd, jnp.concatenate([pair, pair], axis=0), 0.0)

    n_chunks = rows // L
    n_pairs = r.shape[1] // LANES
    at, rt, bt, kt, bh, kh, vv, g_last = [], [], [], [], [], [], [], []
    for c in range(n_chunks):
        sl = slice(c * L, (c + 1) * L)
        ldc = ld[sl]
        d1 = ldc.astype(bf16)
        d2 = (ldc - d1.astype(f32)).astype(bf16)
        d3 = (ldc - d1.astype(f32) - d2.astype(f32)).astype(bf16)
        cs = (jnp.dot(tri, d1, preferred_element_type=f32) + jnp.dot(tri, d2, preferred_element_type=f32)
              + jnp.dot(tri, d3, preferred_element_type=f32))
        last = jnp.sum(jnp.where(last_row, cs, 0.0), axis=0, keepdims=True)
        g_inv = jnp.exp(-cs)
        g_rel = jnp.exp(last - cs)
        at += _slabs(a_in[sl] * jnp.exp(cs - ldc))
        rt += _slabs(r[sl] * jnp.exp(cs))
        bt += _slabs(b_in[sl] * g_inv)
        kt += _slabs(k[sl] * g_inv)
        bh += _slabs(b_in[sl] * g_rel)
        kh += _slabs(k[sl] * g_rel)
        vv += _slabs(v[sl])
        g_last += _slabs(jnp.exp(last))
    nu = range(n_chunks * n_pairs)
    gram = [_dot_nt(jnp.concatenate([at[u], rt[u]], axis=0),
                    jnp.concatenate([_stack2(bt[u], lo), _stack2(kt[u], lo)], axis=0)) for u in nu]
    a_ab = [jnp.where(strict, gram[u][:L, :2 * L], 0.0) for u in nu]
    a_ak = [jnp.where(strict, gram[u][:L, 2 * L:], 0.0) for u in nu]
    a_rb = [jnp.where(incl, gram[u][L:, :2 * L], 0.0) for u in nu]
    a_rk = [jnp.where(incl, gram[u][L:, 2 * L:], 0.0) for u in nu]
    v2 = [_stack2(vv[u], lo) for u in nu]
    av = [_dot(a_ak[u], v2[u]) for u in nu]
    n = a_ab
    t = [eye_pair + n[u] for u in nu]
    span = 1
    while 2 * span < L:
        n = [_dot(n[u], blockdiag(n[u])) for u in nu]
        t = [t[u] + _dot(t[u], blockdiag(n[u])) for u in nu]
        span *= 2
    pq = [_dot(t[u], jnp.concatenate([_stack2(at[u], lo), _stack2(av[u], lo)], axis=1)) for u in nu]
    zeros2 = jnp.zeros((2 * L, LANES), f32)
    zeros1 = jnp.zeros((L, LANES), f32)
    yy, mn = [], []
    for u in nu:
        p, q = pq[u][:, :LANES], pq[u][:, LANES:]
        yy.append(_dot(jnp.concatenate([a_rb[u], a_rk[u]], axis=1),
                       jnp.concatenate([jnp.concatenate([_stack2(p, lo), _stack2(q, lo)], axis=1),
                                        jnp.concatenate([zeros2, v2[u]], axis=1)], axis=0)))
        bk_t = jnp.concatenate([bh[u], kh[u]], axis=0).T
        mn.append(_dot(bk_t, jnp.concatenate([pq[u], jnp.concatenate([zeros1, vv[u]], axis=1)], axis=0)))
    hs = [h_ref[p] for p in range(n_pairs)]
    y_chunks = []
    for c in range(n_chunks):
        ys = []
        for p in range(n_pairs):
            u = c * n_pairs + p
            y1 = rt[u] + yy[u][:, :LANES]
            m = jnp.where(eye, g_last[u], 0.0) + jnp.where(bd, mn[u][:, :LANES], 0.0)
            yh = _dot(jnp.concatenate([y1, m], axis=0), hs[p])
            ys.append(yh[:L] + yy[u][:, LANES:])
            hs[p] = yh[L:] + jnp.where(bd, mn[u][:, LANES:], 0.0)
        y_chunks.append(jnp.concatenate(ys, axis=1))
    for p in range(n_pairs):
        h_ref[p] = hs[p]
    y = jnp.concatenate(y_chunks, axis=0)

    mu = _pair_sum(y) * (1.0 / RWKV_HEAD)
    yc = y - mu
    var = _pair_sum(yc * yc) * (1.0 / RWKV_HEAD)
    yn = yc * lax.rsqrt(var + GN_EPS) * lng_ref[...] + lnb_ref[...]
    o_ref[0] = ((yn + bonus) * gate).astype(o_ref.dtype)


def _rwkv7(rkv, small, mix_rkv, mix_small, w0, w2p, a0, a2p, g2p, k_k, k_a, r_k, ln_g, ln_b, *, tm, chunk=CHUNK):
    batch, seq, _ = rkv.shape
    assert seq % tm == 0 and tm % chunk == 0
    width = RWKV_WIDTH
    col = lambda j: pl.BlockSpec((1, tm, width), lambda b, s: (b, s, j))
    vec = lambda j=0: pl.BlockSpec((1, width), lambda b, s: (0, j))
    full = lambda a: pl.BlockSpec(a.shape, lambda b, s: (0,) * a.ndim)
    carry = pltpu.VMEM((8, width), f32)
    return pl.pallas_call(
        functools.partial(_rwkv_kernel, chunk=chunk),
        grid=(batch, seq // tm),
        in_specs=[col(0), col(1), col(2),
                  pl.BlockSpec((1, tm, SMALL_IN), lambda b, s: (b, s, 0)),
                  vec(0), vec(1), vec(2), full(mix_small),
                  vec(), full(w2p), vec(), full(a2p), full(g2p),
                  vec(), vec(), vec(), vec(), vec()],
        out_specs=pl.BlockSpec((1, tm, width), lambda b, s: (b, s, 0)),
        out_shape=jax.ShapeDtypeStruct((batch, seq, width), bf16),
        scratch_shapes=[pltpu.VMEM((RWKV_PAIRS, LANES, LANES), f32), carry, carry, carry,
                        pltpu.VMEM((8, SMALL_IN), f32)],
        compiler_params=_cparams(("parallel", "arbitrary")),
        name="rwkv7",
    )(rkv, rkv, rkv, small, mix_rkv, mix_rkv, mix_rkv, mix_small,
      w0, w2p, a0, a2p, g2p, k_k, k_a, r_k, ln_g, ln_b)


def _out_proj_kernel(x_ref, a_ref, b_ref, wa_ref, wb_ref, o_ref):
    acc = jnp.dot(a_ref[...], wa_ref[...], preferred_element_type=f32)
    acc = acc + jnp.dot(b_ref[...], wb_ref[...], preferred_element_type=f32)
    o_ref[...] = x_ref[...] + acc


def _out_proj(x, a, b, wa, wb, *, tm, tn):
    m, n = x.shape
    return pl.pallas_call(
        _out_proj_kernel,
        grid=(m // tm, n // tn),
        in_specs=[pl.BlockSpec((tm, tn), lambda i, j: (i, j)),
                  pl.BlockSpec((tm, a.shape[1]), lambda i, j: (i, 0)),
                  pl.BlockSpec((tm, b.shape[1]), lambda i, j: (i, 0)),
                  pl.BlockSpec((wa.shape[0], tn), lambda i, j: (0, j)),
                  pl.BlockSpec((wb.shape[0], tn), lambda i, j: (0, j))],
        out_specs=pl.BlockSpec((tm, tn), lambda i, j: (i, j)),
        out_shape=jax.ShapeDtypeStruct((m, n), f32),
        compiler_params=_cparams(("parallel", "parallel")),
        name="out_proj",
    )(x, a, b, wa, wb)


def _ffn_kernel(x_ref, g_ref, wg_ref, wu_ref, wd_ref, o_ref, h_ref, acc_ref):
    j = pl.program_id(1)

    @pl.when(j == 0)
    def _():
        h_ref[...] = _rms(x_ref[...], g_ref[...]).astype(bf16)
        acc_ref[...] = jnp.zeros_like(acc_ref)

    h = h_ref[...]
    gate = jnp.dot(h, wg_ref[...], preferred_element_type=f32)
    up = jnp.dot(h, wu_ref[...], preferred_element_type=f32)
    act = (gate / (1.0 + jnp.exp(-gate))) * up
    acc_ref[...] += jnp.dot(act.astype(bf16), wd_ref[...], preferred_element_type=f32)

    @pl.when(j == pl.num_programs(1) - 1)
    def _():
        o_ref[...] = x_ref[...] + acc_ref[...]


def _ffn(x, g, wg, wu, wd, *, tm, tf):
    m, d = x.shape
    hidden = wg.shape[1]
    assert m % tm == 0 and hidden % tf == 0
    return pl.pallas_call(
        _ffn_kernel,
        grid=(m // tm, hidden // tf),
        in_specs=[pl.BlockSpec((tm, d), lambda i, j: (i, 0)),
                  pl.BlockSpec((1, d), lambda i, j: (0, 0)),
                  pl.BlockSpec((d, tf), lambda i, j: (0, j)),
                  pl.BlockSpec((d, tf), lambda i, j: (0, j)),
                  pl.BlockSpec((tf, d), lambda i, j: (j, 0))],
        out_specs=pl.BlockSpec((tm, d), lambda i, j: (i, 0)),
        out_shape=jax.ShapeDtypeStruct((m, d), f32),
        scratch_shapes=[pltpu.VMEM((tm, d), bf16), pltpu.VMEM((tm, d), f32)],
        compiler_params=_cparams(("parallel", "arbitrary")),
        name="ffn",
    )(x, g, wg, wu, wd)


def _pad_cols(a, width):
    return jnp.pad(a, ((0, 0), (0, width - a.shape[1])))


def _pad_rows(a, rows, before=0):
    return jnp.pad(a, ((before, rows - before - a.shape[0]), (0, 0)))


def _rope_tables(seq):
    inv_freq = 1.0 / (ROPE_THETA ** (jnp.arange(0, MLA_ROPE, 2, dtype=f32) / MLA_ROPE))
    ang = jnp.arange(seq, dtype=f32)[:, None] * inv_freq[None, :]
    cos, sin = jnp.cos(ang), jnp.sin(ang)
    cos_t = jnp.tile(jnp.concatenate([cos, cos], axis=1), (1, MLA_HEADS))
    sin_t = jnp.tile(jnp.concatenate([-sin, sin], axis=1), (1, MLA_HEADS))
    return cos_t, sin_t


def _layer(x, attn_norm_g, w_in, q_lat_norm, w_uq, kv_lat_norm, w_ukv, q_head_norm, k_nope_norm, k_rope_norm,
           shift_mix, w0, w2, a0, a2, g2, k_k, k_a, r_k, ln_g, ln_b, w_out, ffn_norm_g, w_gate, w_up, w_down,
           cos_t, sin_t, *, tiles):
    batch, seq, d_model = x.shape
    tokens = batch * seq
    row = lambda a: a.reshape(1, -1)
    x2d = x.reshape(tokens, d_model)

    c_kr = MLA_Q_RANK + MLA_KV_RANK + MLA_ROPE
    c_v = c_kr + 3 * RWKV_WIDTH
    w_lat = jnp.concatenate([_pad_cols(w_in[:, :c_kr], MLA_IN), _pad_cols(w_in[:, c_v:], SMALL_IN)], axis=1).astype(bf16)
    w_rkv = w_in[:, c_kr:c_v].astype(bf16)
    mix_rkv = row(shift_mix[:3 * RWKV_WIDTH])
    mix_small = _pad_cols(row(shift_mix[3 * RWKV_WIDTH:]), SMALL_IN)

    lat, small = _norm_matmul(x2d, row(attn_norm_g), w_lat, tm=tiles["in_tm"], tn=MLA_IN + SMALL_IN,
                              splits=(MLA_IN, SMALL_IN), out_dtypes=(bf16, f32))
    (rkv,) = _norm_matmul(x2d, row(attn_norm_g), w_rkv, tm=tiles["in_tm"], tn=tiles["in_tn"],
                          splits=(tiles["in_tn"],), out_dtypes=(f32,))

    wuq = w_uq.reshape(MLA_Q_RANK, MLA_HEADS, MLA_QK)
    wuq = jnp.concatenate([wuq[:, :, :MLA_NOPE].reshape(MLA_Q_RANK, -1),
                           wuq[:, :, MLA_NOPE:].reshape(MLA_Q_RANK, -1)], axis=1).astype(bf16)
    wukv = w_ukv.reshape(MLA_KV_RANK, MLA_HEADS, MLA_NOPE + MLA_V)
    wukv = jnp.concatenate([wukv[:, :, :MLA_NOPE].reshape(MLA_KV_RANK, -1),
                            wukv[:, :, MLA_NOPE:].reshape(MLA_KV_RANK, -1)], axis=1).astype(bf16)
    gqn = row(q_head_norm[:MLA_NOPE])
    gqp = row(jnp.tile(q_head_norm[MLA_NOPE:], MLA_HEADS))
    gkr = _pad_cols(row(k_rope_norm), LANES)
    q, k, vt = _mla_prep(lat, row(q_lat_norm), wuq, row(kv_lat_norm), wukv, gqn, gqp, row(k_nope_norm), gkr,
                         cos_t, sin_t, batch=batch, seq=seq, tm=tiles["tk"])
    out_a = _flash_attention(q, k, vt, tq=tiles["tq"], tk=tiles["tk"])

    w2p = _pad_rows(w2, LANES).astype(bf16)
    a2p = _pad_rows(a2, LANES, before=DECAY_LORA).astype(bf16)
    g2p = _pad_rows(g2, 2 * LANES).astype(bf16)
    out_b = _rwkv7(rkv.reshape(batch, seq, 3 * RWKV_WIDTH), small.reshape(batch, seq, SMALL_IN),
                   mix_rkv, mix_small, row(w0), w2p, row(a0), a2p, g2p, row(k_k), row(k_a), row(r_k),
                   row(ln_g), row(ln_b), tm=tiles["rw_tm"])

    mla_width = MLA_HEADS * MLA_V
    x2 = _out_proj(x2d, out_a.reshape(tokens, mla_width), out_b.reshape(tokens, RWKV_WIDTH),
                   w_out[:mla_width].astype(bf16), w_out[mla_width:].astype(bf16),
                   tm=tiles["out_tm"], tn=tiles["out_tn"])
    y = _ffn(x2, row(ffn_norm_g), w_gate.astype(bf16), w_up.astype(bf16), w_down.astype(bf16),
             tm=tiles["ffn_tm"], tf=tiles["ffn_tf"])
    return y.reshape(batch, seq, d_model)


def _tiles(seq):
    t = lambda want: math.gcd(want, seq)
    return dict(in_tm=t(512), in_tn=1024, tq=t(1024), tk=t(1024) // 2, rw_tm=t(128),
                out_tm=t(512), out_tn=1024, ffn_tm=t(512), ffn_tf=512)


def kernel(x, attn_norm_g, w_in, q_lat_norm, w_uq, kv_lat_norm, w_ukv, q_head_norm, k_nope_norm, k_rope_norm, rwkv_shift_mix, rwkv_w0, rwkv_w2, rwkv_a0, rwkv_a2, rwkv_g2, rwkv_k_k, rwkv_k_a, rwkv_r_k, rwkv_ln_g, rwkv_ln_b, w_out, ffn_norm_g, w_gate, w_up, w_down):
    seq = x.shape[1]
    cos_t, sin_t = _rope_tables(seq)
    tiles = _tiles(seq)
    params = (attn_norm_g, w_in, q_lat_norm, w_uq, kv_lat_norm, w_ukv, q_head_norm, k_nope_norm, k_rope_norm,
              rwkv_shift_mix, rwkv_w0, rwkv_w2, rwkv_a0, rwkv_a2, rwkv_g2, rwkv_k_k, rwkv_k_a,
              rwkv_r_k.reshape(rwkv_r_k.shape[0], -1), rwkv_ln_g, rwkv_ln_b, w_out, ffn_norm_g, w_gate, w_up, w_down)
    for l in range(attn_norm_g.shape[0]):
        x = _layer(x, *(p[l] for p in params), cos_t, sin_t, tiles=tiles)
    return x
```

```python
import functools
import math

import jax
import jax.numpy as jnp
from jax import lax
from jax.experimental import pallas as pl
from jax.experimental.pallas import tpu as pltpu

LANES = 128
HALF = LANES // 2

MLA_HEADS = 8
MLA_Q_RANK = 768
MLA_KV_RANK = 512
MLA_NOPE = 128
MLA_ROPE = 64
MLA_QK = MLA_NOPE + MLA_ROPE
MLA_V = 128
ROPE_THETA = 10000.0
RWKV_HEAD = 64
RWKV_WIDTH = 1024
RWKV_PAIRS = RWKV_WIDTH // LANES
DECAY_LORA = 64
AAA_LORA = 64
GATE_LORA = 160
NORM_EPS = 1e-6
GN_EPS = 64e-5

MLA_IN = MLA_Q_RANK + MLA_KV_RANK + LANES
SMALL_IN = 3 * LANES
CHUNK = 64

VMEM_LIMIT = 56 * 1024 * 1024

f32 = jnp.float32
bf16 = jnp.bfloat16


def _cparams(sem):
    return pltpu.CompilerParams(dimension_semantics=sem, vmem_limit_bytes=VMEM_LIMIT)


def _rms(x, g):
    return x * lax.rsqrt(jnp.mean(x * x, axis=-1, keepdims=True) + NORM_EPS) * g


def _dot(a, b):
    return jnp.dot(a.astype(bf16), b.astype(bf16), preferred_element_type=f32)


def _dot_nt(a, b):
    return lax.dot_general(a.astype(bf16), b.astype(bf16), (((1,), (1,)), ((), ())),
                           preferred_element_type=f32)


def _norm_matmul_kernel(x_ref, g_ref, w_ref, *rest, splits):
    o_refs, h_ref = rest[:-1], rest[-1]

    @pl.when(pl.program_id(1) == 0)
    def _():
        h_ref[...] = _rms(x_ref[...], g_ref[...]).astype(bf16)

    acc = jnp.dot(h_ref[...], w_ref[...], preferred_element_type=f32)
    off = 0
    for o_ref, width in zip(o_refs, splits):
        o_ref[...] = acc[:, off:off + width].astype(o_ref.dtype)
        off += width


def _norm_matmul(x, g, w, *, tm, tn, splits, out_dtypes):
    m, k = x.shape
    n = w.shape[1]
    assert m % tm == 0 and n % tn == 0 and sum(splits) == tn
    assert len(splits) == 1 or tn == n
    out_shape = [jax.ShapeDtypeStruct((m, (n // tn) * s), dt) for s, dt in zip(splits, out_dtypes)]
    return pl.pallas_call(
        functools.partial(_norm_matmul_kernel, splits=splits),
        grid=(m // tm, n // tn),
        in_specs=[pl.BlockSpec((tm, k), lambda i, j: (i, 0)),
                  pl.BlockSpec((1, k), lambda i, j: (0, 0)),
                  pl.BlockSpec((k, tn), lambda i, j: (0, j))],
        out_specs=[pl.BlockSpec((tm, s), lambda i, j: (i, j)) for s in splits],
        out_shape=out_shape,
        scratch_shapes=[pltpu.VMEM((tm, k), bf16)],
        compiler_params=_cparams(("parallel", "arbitrary")),
        name="norm_matmul",
    )(x, g, w)


def _swap_halves(x, period):
    n = x.shape[-1]
    lane = lax.broadcasted_iota(jnp.int32, x.shape, x.ndim - 1)
    first = (lane % period) < (period // 2)
    return jnp.where(first, pltpu.roll(x, n - period // 2, x.ndim - 1), pltpu.roll(x, period // 2, x.ndim - 1))


def _lanes64(slab, hi):
    if hi:
        slab = pltpu.roll(slab, HALF, 1)
    return slab[:, :HALF]


def _mla_prep_kernel(lat_ref, gq_ref, wuq_ref, gkv_ref, wukv_ref, gqn_ref, gqp_ref, gkn_ref, gkr_ref,
                     cosq_ref, sinq_ref, q_ref, k_ref, v_ref):
    lat = lat_ref[...].astype(f32)
    scale = MLA_QK ** -0.5 * math.log2(math.e)
    rows = lat.shape[0]
    lane = lax.broadcasted_iota(jnp.int32, (rows, LANES), 1)
    lo = lane < HALF

    q = _dot(_rms(lat[:, :MLA_Q_RANK], gq_ref[...]), wuq_ref[...])
    pe = q[:, MLA_HEADS * MLA_NOPE:]
    pe_sq = pe * pe
    inv = []
    for h in range(MLA_HEADS):
        nope = q[:, h * MLA_NOPE:(h + 1) * MLA_NOPE]
        slab = pe_sq[:, (h // 2) * LANES:(h // 2 + 1) * LANES]
        ss = jnp.sum(nope * nope, axis=-1, keepdims=True)
        mine = lo if h % 2 == 0 else jnp.logical_not(lo)
        ss = ss + jnp.sum(jnp.where(mine, slab, 0.0), axis=-1, keepdims=True)
        inv.append(lax.rsqrt(ss * (1.0 / MLA_QK) + NORM_EPS))
    inv_pe = jnp.concatenate(
        [jnp.where(lo, inv[2 * p], inv[2 * p + 1]) for p in range(MLA_HEADS // 2)], axis=1)
    pe = pe * inv_pe * gqp_ref[...]
    pe = (pe * cosq_ref[...] + _swap_halves(pe, MLA_ROPE) * sinq_ref[...]) * scale
    for h in range(MLA_HEADS):
        nope = q[:, h * MLA_NOPE:(h + 1) * MLA_NOPE] * (inv[h] * scale) * gqn_ref[...]
        q_ref[0, h, :, :MLA_NOPE] = nope.astype(q_ref.dtype)
        slab = pe[:, (h // 2) * LANES:(h // 2 + 1) * LANES]
        q_ref[0, h, :, MLA_NOPE:] = _lanes64(slab, h % 2 == 1).astype(q_ref.dtype)

    kr = lat[:, MLA_Q_RANK + MLA_KV_RANK:]
    kr = kr * lax.rsqrt(jnp.sum(kr * kr, axis=-1, keepdims=True) * (1.0 / MLA_ROPE) + NORM_EPS) * gkr_ref[...]
    kr = kr * cosq_ref[:, :LANES] + _swap_halves(kr, MLA_ROPE) * sinq_ref[:, :LANES]
    kr = kr[:, :HALF].astype(k_ref.dtype)

    kv = _dot(_rms(lat[:, MLA_Q_RANK:MLA_Q_RANK + MLA_KV_RANK], gkv_ref[...]), wukv_ref[...])
    for h in range(MLA_HEADS):
        kn = kv[:, h * MLA_NOPE:(h + 1) * MLA_NOPE]
        k_ref[0, h, :, :MLA_NOPE] = _rms(kn, gkn_ref[...]).astype(k_ref.dtype)
        k_ref[0, h, :, MLA_NOPE:] = kr
        vh = kv[:, (MLA_HEADS + h) * MLA_V:(MLA_HEADS + h + 1) * MLA_V]
        v_ref[0, h, 0] = vh.T.astype(v_ref.dtype)


def _mla_prep(lat, gq, wuq, gkv, wukv, gqn, gqp, gkn, gkr, cosq, sinq, *, batch, seq, tm):
    nsb = seq // tm
    full = lambda a: pl.BlockSpec(a.shape, lambda b, s: (0,) * a.ndim)
    qk_spec = pl.BlockSpec((1, MLA_HEADS, tm, MLA_QK), lambda b, s: (b, 0, s, 0))
    return pl.pallas_call(
        _mla_prep_kernel,
        grid=(batch, nsb),
        in_specs=[pl.BlockSpec((tm, MLA_IN), lambda b, s: (b * nsb + s, 0)),
                  full(gq), full(wuq), full(gkv), full(wukv), full(gqn), full(gqp), full(gkn), full(gkr),
                  pl.BlockSpec((tm, cosq.shape[1]), lambda b, s: (s, 0)),
                  pl.BlockSpec((tm, sinq.shape[1]), lambda b, s: (s, 0))],
        out_specs=[qk_spec, qk_spec,
                   pl.BlockSpec((1, MLA_HEADS, 1, MLA_V, tm), lambda b, s: (b, 0, s, 0, 0))],
        out_shape=[jax.ShapeDtypeStruct((batch, MLA_HEADS, seq, MLA_QK), bf16),
                   jax.ShapeDtypeStruct((batch, MLA_HEADS, seq, MLA_QK), bf16),
                   jax.ShapeDtypeStruct((batch, MLA_HEADS, nsb, MLA_V, tm), bf16)],
        compiler_params=_cparams(("parallel", "parallel")),
        name="mla_prep",
    )(lat, gq, wuq, gkv, wukv, gqn, gqp, gkn, gkr, cosq, sinq)


def _flash_kernel(q_ref, k_ref, vt_ref, o_ref, *, tq, tk):
    assert tq == 2 * tk
    qi = pl.program_id(2)
    halves = (q_ref[0, 0, :tk, :], q_ref[0, 0, tk:, :])

    def scores(j, q):
        k = k_ref[0, 0, pl.ds(pl.multiple_of(j * tk, tk), tk), :]
        return lax.dot_general(k, q, (((1,), (1,)), ((), ())), preferred_element_type=f32)

    def update(j, s, carry, diagonal):
        m, l, acc = carry
        if diagonal:
            s = jnp.where(lax.broadcasted_iota(jnp.int32, s.shape, 0) <= lax.broadcasted_iota(jnp.int32, s.shape, 1),
                          s, -jnp.inf)
        m_new = jnp.maximum(m, jnp.max(s, axis=0, keepdims=True))
        p = jnp.exp2(s - m_new)
        alpha = jnp.exp2(m - m_new)
        l = alpha * l + jnp.sum(p, axis=0, keepdims=True)
        acc = alpha * acc + jnp.dot(vt_ref[0, 0, j], p.astype(bf16), preferred_element_type=f32)
        return m_new, l, acc

    def run(units, carry):
        carry = list(carry)
        ahead = 2
        s = [scores(j, halves[h]) for j, h, _ in units[:ahead]]
        for u, (j, h, diagonal) in enumerate(units):
            if u + ahead < len(units):
                s.append(scores(units[u + ahead][0], halves[units[u + ahead][1]]))
            carry[h] = update(j, s[u], carry[h], diagonal)
        return tuple(carry)

    def blocks(first_block, count):
        return lambda i, carry: run([(first_block + count * i + d, h, False) for d in range(count) for h in (0, 1)],
                                    carry)

    init = (jnp.full((1, tk), -jnp.inf, f32), jnp.zeros((1, tk), f32), jnp.zeros((MLA_V, tk), f32))
    carry = lax.fori_loop(0, qi // 2, blocks(0, 4), (init, init))
    carry = lax.fori_loop(0, qi % 2, blocks(4 * (qi // 2), 2), carry)
    first, second = run([(2 * qi, 0, True), (2 * qi, 1, False), (2 * qi + 1, 1, True)], carry)
    for half, (_, l, acc) in enumerate((first, second)):
        o_ref[0, half * tk:(half + 1) * tk, :] = (acc / l).T.astype(o_ref.dtype)


def _flash_attention(q, k, vt, *, tq, tk):
    batch, heads, seq, _ = q.shape
    assert seq % tq == 0 and tq == 2 * tk and vt.shape == (batch, heads, seq // tk, MLA_V, tk)
    return pl.pallas_call(
        functools.partial(_flash_kernel, tq=tq, tk=tk),
        grid=(batch, heads, seq // tq),
        in_specs=[pl.BlockSpec((1, 1, tq, MLA_QK), lambda b, h, i: (b, h, i, 0)),
                  pl.BlockSpec((1, 1, seq, MLA_QK), lambda b, h, i: (b, h, 0, 0)),
                  pl.BlockSpec((1, 1, seq // tk, MLA_V, tk), lambda b, h, i: (b, h, 0, 0, 0))],
        out_specs=pl.BlockSpec((1, tq, MLA_V), lambda b, h, i: (b, i, h)),
        out_shape=jax.ShapeDtypeStruct((batch, seq, heads * MLA_V), bf16),
        compiler_params=_cparams(("parallel", "parallel", "arbitrary")),
        name="flash_attention",
    )(q, k, vt)


def _slabs(x):
    return [x[:, p * LANES:(p + 1) * LANES] for p in range(x.shape[1] // LANES)]


def _pair_sum(x):
    lo = lax.broadcasted_iota(jnp.int32, (x.shape[0], LANES), 1) < HALF
    out = []
    for s in _slabs(x):
        s_lo = jnp.sum(jnp.where(lo, s, 0.0), axis=-1, keepdims=True)
        s_hi = jnp.sum(jnp.where(lo, 0.0, s), axis=-1, keepdims=True)
        out.append(jnp.where(lo, s_lo, s_hi))
    return jnp.concatenate(out, axis=1)


def _stack2(x, lo):
    return jnp.concatenate([jnp.where(lo, x, 0.0), jnp.where(lo, 0.0, x)], axis=0)


def _shift_rows(x, carry_ref):
    row = lax.broadcasted_iota(jnp.int32, x.shape, 0)
    prev = jnp.where(row == 0, carry_ref[7:8, :], pltpu.roll(x, 1, 0))
    carry_ref[...] = x[x.shape[0] - 8:, :]
    return prev


def _rwkv_kernel(r_ref, k_ref, v_ref, sm_ref, mixr_ref, mixk_ref, mixv_ref, mixs_ref,
                 w0_ref, w2_ref, a0_ref, a2_ref, g2_ref, kk_ref, ka_ref, rk_ref, lng_ref, lnb_ref,
                 o_ref, h_ref, cr_ref, ck_ref, cv_ref, cs_ref, *, chunk):
    @pl.when(pl.program_id(1) == 0)
    def _():
        h_ref[...] = jnp.zeros_like(h_ref)
        cr_ref[...] = jnp.zeros_like(cr_ref)
        ck_ref[...] = jnp.zeros_like(ck_ref)
        cv_ref[...] = jnp.zeros_like(cv_ref)
        cs_ref[...] = jnp.zeros_like(cs_ref)

    rows = r_ref.shape[1]
    L = chunk

    def mixed(x, carry_ref, mix_ref):
        return x + (_shift_rows(x, carry_ref) - x) * mix_ref[...]

    r = mixed(r_ref[0].astype(f32), cr_ref, mixr_ref)
    k = mixed(k_ref[0].astype(f32), ck_ref, mixk_ref)
    v = mixed(v_ref[0].astype(f32), cv_ref, mixv_ref)
    sm = mixed(sm_ref[0], cs_ref, mixs_ref)

    z = w0_ref[...] + _dot(jnp.tanh(sm[:, :LANES]), w2_ref[...])
    softplus = jnp.maximum(-z, 0.0) + jnp.log(1.0 + jnp.exp(-jnp.abs(z)))
    ld = -jnp.exp(-softplus - 0.5)
    a_sig = 1.0 / (1.0 + jnp.exp(-(a0_ref[...] + _dot(sm[:, :LANES], a2_ref[...]))))
    gate = _dot(1.0 / (1.0 + jnp.exp(-sm[:, LANES:])), g2_ref[...])

    kk = k * kk_ref[...]
    kk = kk * lax.rsqrt(jnp.maximum(_pair_sum(kk * kk), 1e-24))
    k = k * (1.0 + (a_sig - 1.0) * ka_ref[...])
    a_in = -kk
    b_in = kk * a_sig
    bonus = _pair_sum(r * k * rk_ref[...]) * v

    t_i = lax.broadcasted_iota(jnp.int32, (L, 2 * L), 0)
    s_i = lax.broadcasted_iota(jnp.int32, (L, 2 * L), 1) % L
    strict = s_i < t_i
    incl = s_i <= t_i
    tri = (lax.broadcasted_iota(jnp.int32, (L, L), 1) <= lax.broadcasted_iota(jnp.int32, (L, L), 0)).astype(bf16)
    rr = lax.broadcasted_iota(jnp.int32, (2 * L, 2 * L), 0)
    cc = lax.broadcasted_iota(jnp.int32, (2 * L, 2 * L), 1)
    bd = (rr < L) == (cc < L)
    eye = rr == cc
    eye_pair = jnp.where(eye[:L], 1.0, 0.0) + jnp.where(eye[L:], 1.0, 0.0)
    lo = lax.broadcasted_iota(jnp.int32, (L, LANES), 1) < HALF
    last_row = lax.broadcasted_iota(jnp.int32, (L, r.shape[1]), 0) == L - 1

    def blockdiag(pair):
        return jnp.where(bd, jnp.concatenate([pair, pair], axis=0), 0.0)

    n_chunks = rows // L
    n_pairs = r.shape[1] // LANES
    at, rt, bt, kt, bh, kh, vv, g_last = [], [], [], [], [], [], [], []
    for c in range(n_chunks):
        sl = slice(c * L, (c + 1) * L)
        ldc = ld[sl]
        d1 = ldc.astype(bf16)
        d2 = (ldc - d1.astype(f32)).astype(bf16)
        d3 = (ldc - d1.astype(f32) - d2.astype(f32)).astype(bf16)
        cs = (jnp.dot(tri, d1, preferred_element_type=f32) + jnp.dot(tri, d2, preferred_element_type=f32)
              + jnp.dot(tri, d3, preferred_element_type=f32))
        last = jnp.sum(jnp.where(last_row, cs, 0.0), axis=0, keepdims=True)
        g_inv = jnp.exp(-cs)
        g_rel = jnp.exp(last - cs)
        at += _slabs(a_in[sl] * jnp.exp(cs - ldc))
        rt += _slabs(r[sl] * jnp.exp(cs))
        bt += _slabs(b_in[sl] * g_inv)
        kt += _slabs(k[sl] * g_inv)
        bh += _slabs(b_in[sl] * g_rel)
        kh += _slabs(k[sl] * g_rel)
        vv += _slabs(v[sl])
        g_last += _slabs(jnp.exp(last))
    nu = range(n_chunks * n_pairs)
    gram = [_dot_nt(jnp.concatenate([at[u], rt[u]], axis=0),
                    jnp.concatenate([_stack2(bt[u], lo), _stack2(kt[u], lo)], axis=0)) for u in nu]
    a_ab = [jnp.where(strict, gram[u][:L, :2 * L], 0.0) for u in nu]
    a_ak = [jnp.where(strict, gram[u][:L, 2 * L:], 0.0) for u in nu]
    a_rb = [jnp.where(incl, gram[u][L:, :2 * L], 0.0) for u in nu]
    a_rk = [jnp.where(incl, gram[u][L:, 2 * L:], 0.0) for u in nu]
    v2 = [_stack2(vv[u], lo) for u in nu]
    av = [_dot(a_ak[u], v2[u]) for u in nu]
    n = a_ab
    t = [eye_pair + n[u] for u in nu]
    span = 1
    while 2 * span < L:
        n = [_dot(n[u], blockdiag(n[u])) for u in nu]
        t = [t[u] + _dot(t[u], blockdiag(n[u])) for u in nu]
        span *= 2
    pq = [_dot(t[u], jnp.concatenate([_stack2(at[u], lo), _stack2(av[u], lo)], axis=1)) for u in nu]
    zeros2 = jnp.zeros((2 * L, LANES), f32)
    zeros1 = jnp.zeros((L, LANES), f32)
    yy, mn = [], []
    for u in nu:
        p, q = pq[u][:, :LANES], pq[u][:, LANES:]
        yy.append(_dot(jnp.concatenate([a_rb[u], a_rk[u]], axis=1),
                       jnp.concatenate([jnp.concatenate([_stack2(p, lo), _stack2(q, lo)], axis=1),
                                        jnp.concatenate([zeros2, v2[u]], axis=1)], axis=0)))
        bk_t = jnp.concatenate([bh[u], kh[u]], axis=0).T
        mn.append(_dot(bk_t, jnp.concatenate([pq[u], jnp.concatenate([zeros1, vv[u]], axis=1)], axis=0)))
    hs = [h_ref[p] for p in range(n_pairs)]
    y_chunks = []
    for c in range(n_chunks):
        ys = []
        for p in range(n_pairs):
            u = c * n_pairs + p
            y1 = rt[u] + yy[u][:, :LANES]
            m = jnp.where(eye, g_last[u], 0.0) + jnp.where(bd, mn[u][:, :LANES], 0.0)
            yh = _dot(jnp.concatenate([y1, m], axis=0), hs[p])
            ys.append(yh[:L] + yy[u][:, LANES:])
            hs[p] = yh[L:] + jnp.where(bd, mn[u][:, LANES:], 0.0)
        y_chunks.append(jnp.concatenate(ys, axis=1))
    for p in range(n_pairs):
        h_ref[p] = hs[p]
    y = jnp.concatenate(y_chunks, axis=0)

    mu = _pair_sum(y) * (1.0 / RWKV_HEAD)
    yc = y - mu
    var = _pair_sum(yc * yc) * (1.0 / RWKV_HEAD)
    yn = yc * lax.rsqrt(var + GN_EPS) * lng_ref[...] + lnb_ref[...]
    o_ref[0] = ((yn + bonus) * gate).astype(o_ref.dtype)


def _rwkv7(rkv, small, mix_rkv, mix_small, w0, w2p, a0, a2p, g2p, k_k, k_a, r_k, ln_g, ln_b, *, tm, chunk=CHUNK):
    batch, seq, _ = rkv.shape
    assert seq % tm == 0 and tm % chunk == 0
    width = RWKV_WIDTH
    col = lambda j: pl.BlockSpec((1, tm, width), lambda b, s: (b, s, j))
    vec = lambda j=0: pl.BlockSpec((1, width), lambda b, s: (0, j))
    full = lambda a: pl.BlockSpec(a.shape, lambda b, s: (0,) * a.ndim)
    carry = pltpu.VMEM((8, width), f32)
    return pl.pallas_call(
        functools.partial(_rwkv_kernel, chunk=chunk),
        grid=(batch, seq // tm),
        in_specs=[col(0), col(1), col(2),
                  pl.BlockSpec((1, tm, SMALL_IN), lambda b, s: (b, s, 0)),
                  vec(0), vec(1), vec(2), full(mix_small),
                  vec(), full(w2p), vec(), full(a2p), full(g2p),
                  vec(), vec(), vec(), vec(), vec()],
        out_specs=pl.BlockSpec((1, tm, width), lambda b, s: (b, s, 0)),
        out_shape=jax.ShapeDtypeStruct((batch, seq, width), bf16),
        scratch_shapes=[pltpu.VMEM((RWKV_PAIRS, LANES, LANES), f32), carry, carry, carry,
                        pltpu.VMEM((8, SMALL_IN), f32)],
        compiler_params=_cparams(("parallel", "arbitrary")),
        name="rwkv7",
    )(rkv, rkv, rkv, small, mix_rkv, mix_rkv, mix_rkv, mix_small,
      w0, w2p, a0, a2p, g2p, k_k, k_a, r_k, ln_g, ln_b)


def _out_proj_kernel(x_ref, a_ref, b_ref, wa_ref, wb_ref, o_ref):
    acc = jnp.dot(a_ref[...], wa_ref[...], preferred_element_type=f32)
    acc = acc + jnp.dot(b_ref[...], wb_ref[...], preferred_element_type=f32)
    o_ref[...] = x_ref[...] + acc


def _out_proj(x, a, b, wa, wb, *, tm, tn):
    m, n = x.shape
    return pl.pallas_call(
        _out_proj_kernel,
        grid=(m // tm, n // tn),
        in_specs=[pl.BlockSpec((tm, tn), lambda i, j: (i, j)),
                  pl.BlockSpec((tm, a.shape[1]), lambda i, j: (i, 0)),
                  pl.BlockSpec((tm, b.shape[1]), lambda i, j: (i, 0)),
                  pl.BlockSpec((wa.shape[0], tn), lambda i, j: (0, j)),
                  pl.BlockSpec((wb.shape[0], tn), lambda i, j: (0, j))],
        out_specs=pl.BlockSpec((tm, tn), lambda i, j: (i, j)),
        out_shape=jax.ShapeDtypeStruct((m, n), f32),
        compiler_params=_cparams(("parallel", "parallel")),
        name="out_proj",
    )(x, a, b, wa, wb)


def _ffn_kernel(x_ref, g_ref, wg_ref, wu_ref, wd_ref, o_ref, h_ref, acc_ref):
    j = pl.program_id(1)

    @pl.when(j == 0)
    def _():
        h_ref[...] = _rms(x_ref[...], g_ref[...]).astype(bf16)
        acc_ref[...] = jnp.zeros_like(acc_ref)

    h = h_ref[...]
    gate = jnp.dot(h, wg_ref[...], preferred_element_type=f32)
    up = jnp.dot(h, wu_ref[...], preferred_element_type=f32)
    act = (gate / (1.0 + jnp.exp(-gate))) * up
    acc_ref[...] += jnp.dot(act.astype(bf16), wd_ref[...], preferred_element_type=f32)

    @pl.when(j == pl.num_programs(1) - 1)
    def _():
        o_ref[...] = x_ref[...] + acc_ref[...]


def _ffn(x, g, wg, wu, wd, *, tm, tf):
    m, d = x.shape
    hidden = wg.shape[1]
    assert m % tm == 0 and hidden % tf == 0
    return pl.pallas_call(
        _ffn_kernel,
        grid=(m // tm, hidden // tf),
        in_specs=[pl.BlockSpec((tm, d), lambda i, j: (i, 0)),
                  pl.BlockSpec((1, d), lambda i, j: (0, 0)),
                  pl.BlockSpec((d, tf), lambda i, j: (0, j)),
                  pl.BlockSpec((d, tf), lambda i, j: (0, j)),
                  pl.BlockSpec((tf, d), lambda i, j: (j, 0))],
        out_specs=pl.BlockSpec((tm, d), lambda i, j: (i, 0)),
        out_shape=jax.ShapeDtypeStruct((m, d), f32),
        scratch_shapes=[pltpu.VMEM((tm, d), bf16), pltpu.VMEM((tm, d), f32)],
        compiler_params=_cparams(("parallel", "arbitrary")),
        name="ffn",
    )(x, g, wg, wu, wd)


def _pad_cols(a, width):
    return jnp.pad(a, ((0, 0), (0, width - a.shape[1])))


def _pad_rows(a, rows, before=0):
    return jnp.pad(a, ((before, rows - before - a.shape[0]), (0, 0)))


def _rope_tables(seq):
    inv_freq = 1.0 / (ROPE_THETA ** (jnp.arange(0, MLA_ROPE, 2, dtype=f32) / MLA_ROPE))
    ang = jnp.arange(seq, dtype=f32)[:, None] * inv_freq[None, :]
    cos, sin = jnp.cos(ang), jnp.sin(ang)
    cos_t = jnp.tile(jnp.concatenate([cos, cos], axis=1), (1, MLA_HEADS))
    sin_t = jnp.tile(jnp.concatenate([-sin, sin], axis=1), (1, MLA_HEADS))
    return cos_t, sin_t


def _layer(x, attn_norm_g, w_in, q_lat_norm, w_uq, kv_lat_norm, w_ukv, q_head_norm, k_nope_norm, k_rope_norm,
           shift_mix, w0, w2, a0, a2, g2, k_k, k_a, r_k, ln_g, ln_b, w_out, ffn_norm_g, w_gate, w_up, w_down,
           cos_t, sin_t, *, tiles):
    batch, seq, d_model = x.shape
    tokens = batch * seq
    row = lambda a: a.reshape(1, -1)
    x2d = x.reshape(tokens, d_model)

    c_kr = MLA_Q_RANK + MLA_KV_RANK + MLA_ROPE
    c_v = c_kr + 3 * RWKV_WIDTH
    w_lat = jnp.concatenate([_pad_cols(w_in[:, :c_kr], MLA_IN), _pad_cols(w_in[:, c_v:], SMALL_IN)], axis=1).astype(bf16)
    w_rkv = w_in[:, c_kr:c_v].astype(bf16)
    mix_rkv = row(shift_mix[:3 * RWKV_WIDTH])
    mix_small = _pad_cols(row(shift_mix[3 * RWKV_WIDTH:]), SMALL_IN)

    lat, small = _norm_matmul(x2d, row(attn_norm_g), w_lat, tm=tiles["in_tm"], tn=MLA_IN + SMALL_IN,
                              splits=(MLA_IN, SMALL_IN), out_dtypes=(bf16, f32))
    (rkv,) = _norm_matmul(x2d, row(attn_norm_g), w_rkv, tm=tiles["rkv_tm"], tn=tiles["in_tn"],
                          splits=(tiles["in_tn"],), out_dtypes=(bf16,))

    wuq = w_uq.reshape(MLA_Q_RANK, MLA_HEADS, MLA_QK)
    wuq = jnp.concatenate([wuq[:, :, :MLA_NOPE].reshape(MLA_Q_RANK, -1),
                           wuq[:, :, MLA_NOPE:].reshape(MLA_Q_RANK, -1)], axis=1).astype(bf16)
    wukv = w_ukv.reshape(MLA_KV_RANK, MLA_HEADS, MLA_NOPE + MLA_V)
    wukv = jnp.concatenate([wukv[:, :, :MLA_NOPE].reshape(MLA_KV_RANK, -1),
                            wukv[:, :, MLA_NOPE:].reshape(MLA_KV_RANK, -1)], axis=1).astype(bf16)
    gqn = row(q_head_norm[:MLA_NOPE])
    gqp = row(jnp.tile(q_head_norm[MLA_NOPE:], MLA_HEADS))
    gkr = _pad_cols(row(k_rope_norm), LANES)
    q, k, vt = _mla_prep(lat, row(q_lat_norm), wuq, row(kv_lat_norm), wukv, gqn, gqp, row(k_nope_norm), gkr,
                         cos_t, sin_t, batch=batch, seq=seq, tm=tiles["tk"])
    out_a = _flash_attention(q, k, vt, tq=tiles["tq"], tk=tiles["tk"])

    w2p = _pad_rows(w2, LANES).astype(bf16)
    a2p = _pad_rows(a2, LANES, before=DECAY_LORA).astype(bf16)
    g2p = _pad_rows(g2, 2 * LANES).astype(bf16)
    out_b = _rwkv7(rkv.reshape(batch, seq, 3 * RWKV_WIDTH), small.reshape(batch, seq, SMALL_IN),
                   mix_rkv, mix_small, row(w0), w2p, row(a0), a2p, g2p, row(k_k), row(k_a), row(r_k),
                   row(ln_g), row(ln_b), tm=tiles["rw_tm"])

    mla_width = MLA_HEADS * MLA_V
    x2 = _out_proj(x2d, out_a.reshape(tokens, mla_width), out_b.reshape(tokens, RWKV_WIDTH),
                   w_out[:mla_width].astype(bf16), w_out[mla_width:].astype(bf16),
                   tm=tiles["out_tm"], tn=tiles["out_tn"])
    y = _ffn(x2, row(ffn_norm_g), w_gate.astype(bf16), w_up.astype(bf16), w_down.astype(bf16),
             tm=tiles["ffn_tm"], tf=tiles["ffn_tf"])
    return y.reshape(batch, seq, d_model)


def _tiles(seq):
    t = lambda want: math.gcd(want, seq)
    return dict(in_tm=t(512), rkv_tm=t(1024), in_tn=1024, tq=t(1024), tk=t(1024) // 2, rw_tm=t(128),
                out_tm=t(512), out_tn=2048, ffn_tm=t(512), ffn_tf=512)


def kernel(x, attn_norm_g, w_in, q_lat_norm, w_uq, kv_lat_norm, w_ukv, q_head_norm, k_nope_norm, k_rope_norm, rwkv_shift_mix, rwkv_w0, rwkv_w2, rwkv_a0, rwkv_a2, rwkv_g2, rwkv_k_k, rwkv_k_a, rwkv_r_k, rwkv_ln_g, rwkv_ln_b, w_out, ffn_norm_g, w_gate, w_up, w_down):
    seq = x.shape[1]
    cos_t, sin_t = _rope_tables(seq)
    tiles = _tiles(seq)
    params = (attn_norm_g, w_in, q_lat_norm, w_uq, kv_lat_norm, w_ukv, q_head_norm, k_nope_norm, k_rope_norm,
              rwkv_shift_mix, rwkv_w0, rwkv_w2, rwkv_a0, rwkv_a2, rwkv_g2, rwkv_k_k, rwkv_k_a,
              rwkv_r_k.reshape(rwkv_r_k.shape[0], -1), rwkv_ln_g, rwkv_ln_b, w_out, ffn_norm_g, w_gate, w_up, w_down)
    for l in range(attn_norm_g.shape[0]):
        x = _layer(x, *(p[l] for p in params), cos_t, sin_t, tiles=tiles)
    return x
```

```python
import functools
import math

import jax
import jax.numpy as jnp
from jax import lax
from jax.experimental import pallas as pl
from jax.experimental.pallas import tpu as pltpu

LANES = 128
HALF = LANES // 2

MLA_HEADS = 8
MLA_Q_RANK = 768
MLA_KV_RANK = 512
MLA_NOPE = 128
MLA_ROPE = 64
MLA_QK = MLA_NOPE + MLA_ROPE
MLA_V = 128
ROPE_THETA = 10000.0
RWKV_HEAD = 64
RWKV_WIDTH = 1024
RWKV_PAIRS = RWKV_WIDTH // LANES
DECAY_LORA = 64
AAA_LORA = 64
GATE_LORA = 160
NORM_EPS = 1e-6
GN_EPS = 64e-5

MLA_IN = MLA_Q_RANK + MLA_KV_RANK + LANES
SMALL_IN = 3 * LANES
CHUNK = 64

VMEM_LIMIT = 56 * 1024 * 1024

f32 = jnp.float32
bf16 = jnp.bfloat16


def _cparams(sem):
    return pltpu.CompilerParams(dimension_semantics=sem, vmem_limit_bytes=VMEM_LIMIT)


def _rms(x, g):
    return x * lax.rsqrt(jnp.mean(x * x, axis=-1, keepdims=True) + NORM_EPS) * g


def _dot(a, b):
    return jnp.dot(a.astype(bf16), b.astype(bf16), preferred_element_type=f32)


def _dot_nt(a, b):
    return lax.dot_general(a.astype(bf16), b.astype(bf16), (((1,), (1,)), ((), ())),
                           preferred_element_type=f32)


def _norm_matmul_kernel(x_ref, g_ref, w_ref, h_ref, *o_refs, splits):
    h = _rms(x_ref[...], g_ref[...]).astype(bf16)
    h_ref[...] = h
    acc = jnp.dot(h, w_ref[...], preferred_element_type=f32)
    off = 0
    for o_ref, width in zip(o_refs, splits):
        o_ref[...] = acc[:, off:off + width].astype(o_ref.dtype)
        off += width


def _norm_matmul(x, g, w, *, tm, splits, out_dtypes):
    m, k = x.shape
    n = w.shape[1]
    assert m % tm == 0 and sum(splits) == n
    return pl.pallas_call(
        functools.partial(_norm_matmul_kernel, splits=splits),
        grid=(m // tm,),
        in_specs=[pl.BlockSpec((tm, k), lambda i: (i, 0)),
                  pl.BlockSpec((1, k), lambda i: (0, 0)),
                  pl.BlockSpec((k, n), lambda i: (0, 0))],
        out_specs=[pl.BlockSpec((tm, k), lambda i: (i, 0))] + [pl.BlockSpec((tm, s), lambda i: (i, 0)) for s in splits],
        out_shape=[jax.ShapeDtypeStruct((m, k), bf16)] + [jax.ShapeDtypeStruct((m, s), dt) for s, dt in zip(splits, out_dtypes)],
        compiler_params=_cparams(("parallel",)),
        name="norm_matmul",
    )(x, g, w)


def _matmul_kernel(h_ref, w_ref, o_ref):
    o_ref[...] = jnp.dot(h_ref[...], w_ref[...], preferred_element_type=f32).astype(o_ref.dtype)


def _matmul(h, w, *, tm, tn, out_dtype):
    m, k = h.shape
    n = w.shape[1]
    assert m % tm == 0 and n % tn == 0
    return pl.pallas_call(
        _matmul_kernel,
        grid=(m // tm, n // tn),
        in_specs=[pl.BlockSpec((tm, k), lambda i, j: (i, 0)),
                  pl.BlockSpec((k, tn), lambda i, j: (0, j))],
        out_specs=pl.BlockSpec((tm, tn), lambda i, j: (i, j)),
        out_shape=jax.ShapeDtypeStruct((m, n), out_dtype),
        compiler_params=_cparams(("parallel", "parallel")),
        name="matmul",
    )(h, w)


def _swap_halves(x, period):
    n = x.shape[-1]
    lane = lax.broadcasted_iota(jnp.int32, x.shape, x.ndim - 1)
    first = (lane % period) < (period // 2)
    return jnp.where(first, pltpu.roll(x, n - period // 2, x.ndim - 1), pltpu.roll(x, period // 2, x.ndim - 1))


def _lanes64(slab, hi):
    if hi:
        slab = pltpu.roll(slab, HALF, 1)
    return slab[:, :HALF]


def _mla_prep_kernel(lat_ref, gq_ref, wuq_ref, gkv_ref, wukv_ref, gqn_ref, gqp_ref, gkn_ref, gkr_ref,
                     cosq_ref, sinq_ref, q_ref, k_ref, v_ref):
    lat = lat_ref[...].astype(f32)
    scale = MLA_QK ** -0.5 * math.log2(math.e)
    rows = lat.shape[0]
    lane = lax.broadcasted_iota(jnp.int32, (rows, LANES), 1)
    lo = lane < HALF

    q = _dot(_rms(lat[:, :MLA_Q_RANK], gq_ref[...]), wuq_ref[...])
    pe = q[:, MLA_HEADS * MLA_NOPE:]
    pe_sq = pe * pe
    inv = []
    for h in range(MLA_HEADS):
        nope = q[:, h * MLA_NOPE:(h + 1) * MLA_NOPE]
        slab = pe_sq[:, (h // 2) * LANES:(h // 2 + 1) * LANES]
        ss = jnp.sum(nope * nope, axis=-1, keepdims=True)
        mine = lo if h % 2 == 0 else jnp.logical_not(lo)
        ss = ss + jnp.sum(jnp.where(mine, slab, 0.0), axis=-1, keepdims=True)
        inv.append(lax.rsqrt(ss * (1.0 / MLA_QK) + NORM_EPS))
    inv_pe = jnp.concatenate(
        [jnp.where(lo, inv[2 * p], inv[2 * p + 1]) for p in range(MLA_HEADS // 2)], axis=1)
    pe = pe * inv_pe * gqp_ref[...]
    pe = (pe * cosq_ref[...] + _swap_halves(pe, MLA_ROPE) * sinq_ref[...]) * scale
    for h in range(MLA_HEADS):
        nope = q[:, h * MLA_NOPE:(h + 1) * MLA_NOPE] * (inv[h] * scale) * gqn_ref[...]
        q_ref[0, h, :, :MLA_NOPE] = nope.astype(q_ref.dtype)
        slab = pe[:, (h // 2) * LANES:(h // 2 + 1) * LANES]
        q_ref[0, h, :, MLA_NOPE:] = _lanes64(slab, h % 2 == 1).astype(q_ref.dtype)

    kr = lat[:, MLA_Q_RANK + MLA_KV_RANK:]
    kr = kr * lax.rsqrt(jnp.sum(kr * kr, axis=-1, keepdims=True) * (1.0 / MLA_ROPE) + NORM_EPS) * gkr_ref[...]
    kr = kr * cosq_ref[:, :LANES] + _swap_halves(kr, MLA_ROPE) * sinq_ref[:, :LANES]
    kr = kr[:, :HALF].astype(k_ref.dtype)

    kv = _dot(_rms(lat[:, MLA_Q_RANK:MLA_Q_RANK + MLA_KV_RANK], gkv_ref[...]), wukv_ref[...])
    for h in range(MLA_HEADS):
        kn = kv[:, h * MLA_NOPE:(h + 1) * MLA_NOPE]
        k_ref[0, h, :, :MLA_NOPE] = _rms(kn, gkn_ref[...]).astype(k_ref.dtype)
        k_ref[0, h, :, MLA_NOPE:] = kr
        vh = kv[:, (MLA_HEADS + h) * MLA_V:(MLA_HEADS + h + 1) * MLA_V]
        v_ref[0, h, 0] = vh.T.astype(v_ref.dtype)


def _mla_prep(lat, gq, wuq, gkv, wukv, gqn, gqp, gkn, gkr, cosq, sinq, *, batch, seq, tm):
    nsb = seq // tm
    full = lambda a: pl.BlockSpec(a.shape, lambda b, s: (0,) * a.ndim)
    qk_spec = pl.BlockSpec((1, MLA_HEADS, tm, MLA_QK), lambda b, s: (b, 0, s, 0))
    return pl.pallas_call(
        _mla_prep_kernel,
        grid=(batch, nsb),
        in_specs=[pl.BlockSpec((tm, MLA_IN), lambda b, s: (b * nsb + s, 0)),
                  full(gq), full(wuq), full(gkv), full(wukv), full(gqn), full(gqp), full(gkn), full(gkr),
                  pl.BlockSpec((tm, cosq.shape[1]), lambda b, s: (s, 0)),
                  pl.BlockSpec((tm, sinq.shape[1]), lambda b, s: (s, 0))],
        out_specs=[qk_spec, qk_spec,
                   pl.BlockSpec((1, MLA_HEADS, 1, MLA_V, tm), lambda b, s: (b, 0, s, 0, 0))],
        out_shape=[jax.ShapeDtypeStruct((batch, MLA_HEADS, seq, MLA_QK), bf16),
                   jax.ShapeDtypeStruct((batch, MLA_HEADS, seq, MLA_QK), bf16),
                   jax.ShapeDtypeStruct((batch, MLA_HEADS, nsb, MLA_V, tm), bf16)],
        compiler_params=_cparams(("parallel", "parallel")),
        name="mla_prep",
    )(lat, gq, wuq, gkv, wukv, gqn, gqp, gkn, gkr, cosq, sinq)


def _flash_kernel(q_ref, k_ref, vt_ref, o_ref, *, tq, tk):
    assert tq == 2 * tk
    qi = pl.program_id(2)
    halves = (q_ref[0, 0, :tk, :], q_ref[0, 0, tk:, :])

    def scores(j, q):
        k = k_ref[0, 0, pl.ds(pl.multiple_of(j * tk, tk), tk), :]
        return lax.dot_general(k, q, (((1,), (1,)), ((), ())), preferred_element_type=f32)

    def update(j, s, carry, diagonal):
        m, l, acc = carry
        if diagonal:
            s = jnp.where(lax.broadcasted_iota(jnp.int32, s.shape, 0) <= lax.broadcasted_iota(jnp.int32, s.shape, 1),
                          s, -jnp.inf)
        m_new = jnp.maximum(m, jnp.max(s, axis=0, keepdims=True))
        p = jnp.exp2(s - m_new)
        alpha = jnp.exp2(m - m_new)
        l = alpha * l + jnp.sum(p, axis=0, keepdims=True)
        acc = alpha * acc + jnp.dot(vt_ref[0, 0, j], p.astype(bf16), preferred_element_type=f32)
        return m_new, l, acc

    def run(units, carry):
        carry = list(carry)
        ahead = 2
        s = [scores(j, halves[h]) for j, h, _ in units[:ahead]]
        for u, (j, h, diagonal) in enumerate(units):
            if u + ahead < len(units):
                s.append(scores(units[u + ahead][0], halves[units[u + ahead][1]]))
            carry[h] = update(j, s[u], carry[h], diagonal)
        return tuple(carry)

    def blocks(first_block, count):
        return lambda i, carry: run([(first_block + count * i + d, h, False) for d in range(count) for h in (0, 1)],
                                    carry)

    init = (jnp.full((1, tk), -jnp.inf, f32), jnp.zeros((1, tk), f32), jnp.zeros((MLA_V, tk), f32))
    carry = lax.fori_loop(0, qi // 2, blocks(0, 4), (init, init))
    carry = lax.fori_loop(0, qi % 2, blocks(4 * (qi // 2), 2), carry)
    first, second = run([(2 * qi, 0, True), (2 * qi, 1, False), (2 * qi + 1, 1, True)], carry)
    for half, (_, l, acc) in enumerate((first, second)):
        o_ref[0, half * tk:(half + 1) * tk, :] = (acc / l).T.astype(o_ref.dtype)


def _flash_attention(q, k, vt, *, tq, tk):
    batch, heads, seq, _ = q.shape
    assert seq % tq == 0 and tq == 2 * tk and vt.shape == (batch, heads, seq // tk, MLA_V, tk)
    return pl.pallas_call(
        functools.partial(_flash_kernel, tq=tq, tk=tk),
        grid=(batch, heads, seq // tq),
        in_specs=[pl.BlockSpec((1, 1, tq, MLA_QK), lambda b, h, i: (b, h, i, 0)),
                  pl.BlockSpec((1, 1, seq, MLA_QK), lambda b, h, i: (b, h, 0, 0)),
                  pl.BlockSpec((1, 1, seq // tk, MLA_V, tk), lambda b, h, i: (b, h, 0, 0, 0))],
        out_specs=pl.BlockSpec((1, tq, MLA_V), lambda b, h, i: (b, i, h)),
        out_shape=jax.ShapeDtypeStruct((batch, seq, heads * MLA_V), bf16),
        compiler_params=_cparams(("parallel", "parallel", "arbitrary")),
        name="flash_attention",
    )(q, k, vt)


def _slabs(x):
    return [x[:, p * LANES:(p + 1) * LANES] for p in range(x.shape[1] // LANES)]


def _pair_sum(x):
    lo = lax.broadcasted_iota(jnp.int32, (x.shape[0], LANES), 1) < HALF
    out = []
    for s in _slabs(x):
        s_lo = jnp.sum(jnp.where(lo, s, 0.0), axis=-1, keepdims=True)
        s_hi = jnp.sum(jnp.where(lo, 0.0, s), axis=-1, keepdims=True)
        out.append(jnp.where(lo, s_lo, s_hi))
    return jnp.concatenate(out, axis=1)


def _stack2(x, lo):
    return jnp.concatenate([jnp.where(lo, x, 0.0), jnp.where(lo, 0.0, x)], axis=0)


def _shift_rows(x, carry_ref):
    row = lax.broadcasted_iota(jnp.int32, x.shape, 0)
    prev = jnp.where(row == 0, carry_ref[7:8, :], pltpu.roll(x, 1, 0))
    carry_ref[...] = x[x.shape[0] - 8:, :]
    return prev


def _rwkv_kernel(r_ref, k_ref, v_ref, sm_ref, mixr_ref, mixk_ref, mixv_ref, mixs_ref,
                 w0_ref, w2_ref, a0_ref, a2_ref, g2_ref, kk_ref, ka_ref, rk_ref, lng_ref, lnb_ref,
                 o_ref, h_ref, cr_ref, ck_ref, cv_ref, cs_ref, *, chunk):
    @pl.when(pl.program_id(1) == 0)
    def _():
        h_ref[...] = jnp.zeros_like(h_ref)
        cr_ref[...] = jnp.zeros_like(cr_ref)
        ck_ref[...] = jnp.zeros_like(ck_ref)
        cv_ref[...] = jnp.zeros_like(cv_ref)
        cs_ref[...] = jnp.zeros_like(cs_ref)

    rows = r_ref.shape[1]
    L = chunk

    def mixed(x, carry_ref, mix_ref):
        return x + (_shift_rows(x, carry_ref) - x) * mix_ref[...]

    r = mixed(r_ref[0].astype(f32), cr_ref, mixr_ref)
    k = mixed(k_ref[0].astype(f32), ck_ref, mixk_ref)
    v = mixed(v_ref[0].astype(f32), cv_ref, mixv_ref)
    sm = mixed(sm_ref[0], cs_ref, mixs_ref)

    z = w0_ref[...] + _dot(jnp.tanh(sm[:, :LANES]), w2_ref[...])
    softplus = jnp.maximum(-z, 0.0) + jnp.log(1.0 + jnp.exp(-jnp.abs(z)))
    ld = -jnp.exp(-softplus - 0.5)
    a_sig = 1.0 / (1.0 + jnp.exp(-(a0_ref[...] + _dot(sm[:, :LANES], a2_ref[...]))))
    gate = _dot(1.0 / (1.0 + jnp.exp(-sm[:, LANES:])), g2_ref[...])

    kk = k * kk_ref[...]
    kk = kk * lax.rsqrt(jnp.maximum(_pair_sum(kk * kk), 1e-24))
    k = k * (1.0 + (a_sig - 1.0) * ka_ref[...])
    a_in = -kk
    b_in = kk * a_sig
    bonus = _pair_sum(r * k * rk_ref[...]) * v

    t_i = lax.broadcasted_iota(jnp.int32, (L, 2 * L), 0)
    s_i = lax.broadcasted_iota(jnp.int32, (L, 2 * L), 1) % L
    strict = s_i < t_i
    incl = s_i <= t_i
    tri = (lax.broadcasted_iota(jnp.int32, (L, L), 1) <= lax.broadcasted_iota(jnp.int32, (L, L), 0)).astype(bf16)
    rr = lax.broadcasted_iota(jnp.int32, (2 * L, 2 * L), 0)
    cc = lax.broadcasted_iota(jnp.int32, (2 * L, 2 * L), 1)
    bd = (rr < L) == (cc < L)
    eye = rr == cc
    eye_pair = jnp.where(eye[:L], 1.0, 0.0) + jnp.where(eye[L:], 1.0, 0.0)
    lo = lax.broadcasted_iota(jnp.int32, (L, LANES), 1) < HALF
    last_row = lax.broadcasted_iota(jnp.int32, (L, r.shape[1]), 0) == L - 1

    def blockdiag(pair):
        return jnp.where(bd, jnp.concatenate([pair, pair], axis=0), 0.0)

    n_chunks = rows // L
    n_pairs = r.shape[1] // LANES
    at, rt, bt, kt, bh, kh, vv, g_last = [], [], [], [], [], [], [], []
    for c in range(n_chunks):
        sl = slice(c * L, (c + 1) * L)
        ldc = ld[sl]
        d1 = ldc.astype(bf16)
        d2 = (ldc - d1.astype(f32)).astype(bf16)
        d3 = (ldc - d1.astype(f32) - d2.astype(f32)).astype(bf16)
        cs = (jnp.dot(tri, d1, preferred_element_type=f32) + jnp.dot(tri, d2, preferred_element_type=f32)
              + jnp.dot(tri, d3, preferred_element_type=f32))
        last = jnp.sum(jnp.where(last_row, cs, 0.0), axis=0, keepdims=True)
        g_inv = jnp.exp(-cs)
        g_rel = jnp.exp(last - cs)
        at += _slabs(a_in[sl] * jnp.exp(cs - ldc))
        rt += _slabs(r[sl] * jnp.exp(cs))
        bt += _slabs(b_in[sl] * g_inv)
        kt += _slabs(k[sl] * g_inv)
        bh += _slabs(b_in[sl] * g_rel)
        kh += _slabs(k[sl] * g_rel)
        vv += _slabs(v[sl])
        g_last += _slabs(jnp.exp(last))
    nu = range(n_chunks * n_pairs)
    gram = [_dot_nt(jnp.concatenate([at[u], rt[u]], axis=0),
                    jnp.concatenate([_stack2(bt[u], lo), _stack2(kt[u], lo)], axis=0)) for u in nu]
    a_ab = [jnp.where(strict, gram[u][:L, :2 * L], 0.0) for u in nu]
    a_ak = [jnp.where(strict, gram[u][:L, 2 * L:], 0.0) for u in nu]
    a_rb = [jnp.where(incl, gram[u][L:, :2 * L], 0.0) for u in nu]
    a_rk = [jnp.where(incl, gram[u][L:, 2 * L:], 0.0) for u in nu]
    v2 = [_stack2(vv[u], lo) for u in nu]
    av = [_dot(a_ak[u], v2[u]) for u in nu]
    n = a_ab
    t = [eye_pair + n[u] for u in nu]
    span = 1
    while 2 * span < L:
        n = [_dot(n[u], blockdiag(n[u])) for u in nu]
        t = [t[u] + _dot(t[u], blockdiag(n[u])) for u in nu]
        span *= 2
    pq = [_dot(t[u], jnp.concatenate([_stack2(at[u], lo), _stack2(av[u], lo)], axis=1)) for u in nu]
    zeros2 = jnp.zeros((2 * L, LANES), f32)
    zeros1 = jnp.zeros((L, LANES), f32)
    yy, mn = [], []
    for u in nu:
        p, q = pq[u][:, :LANES], pq[u][:, LANES:]
        yy.append(_dot(jnp.concatenate([a_rb[u], a_rk[u]], axis=1),
                       jnp.concatenate([jnp.concatenate([_stack2(p, lo), _stack2(q, lo)], axis=1),
                                        jnp.concatenate([zeros2, v2[u]], axis=1)], axis=0)))
        bk_t = jnp.concatenate([bh[u], kh[u]], axis=0).T
        mn.append(_dot(bk_t, jnp.concatenate([pq[u], jnp.concatenate([zeros1, vv[u]], axis=1)], axis=0)))
    hs = [h_ref[p] for p in range(n_pairs)]
    y_chunks = []
    for c in range(n_chunks):
        ys = []
        for p in range(n_pairs):
            u = c * n_pairs + p
            y1 = rt[u] + yy[u][:, :LANES]
            m = jnp.where(eye, g_last[u], 0.0) + jnp.where(bd, mn[u][:, :LANES], 0.0)
            yh = _dot(jnp.concatenate([y1, m], axis=0), hs[p])
            ys.append(yh[:L] + yy[u][:, LANES:])
            hs[p] = yh[L:] + jnp.where(bd, mn[u][:, LANES:], 0.0)
        y_chunks.append(jnp.concatenate(ys, axis=1))
    for p in range(n_pairs):
        h_ref[p] = hs[p]
    y = jnp.concatenate(y_chunks, axis=0)

    mu = _pair_sum(y) * (1.0 / RWKV_HEAD)
    yc = y - mu
    var = _pair_sum(yc * yc) * (1.0 / RWKV_HEAD)
    yn = yc * lax.rsqrt(var + GN_EPS) * lng_ref[...] + lnb_ref[...]
    o_ref[0] = ((yn + bonus) * gate).astype(o_ref.dtype)


def _rwkv7(rkv, small, mix_rkv, mix_small, w0, w2p, a0, a2p, g2p, k_k, k_a, r_k, ln_g, ln_b, *, tm, chunk=CHUNK):
    batch, seq, _ = rkv.shape
    assert seq % tm == 0 and tm % chunk == 0
    width = RWKV_WIDTH
    col = lambda j: pl.BlockSpec((1, tm, width), lambda b, s: (b, s, j))
    vec = lambda j=0: pl.BlockSpec((1, width), lambda b, s: (0, j))
    full = lambda a: pl.BlockSpec(a.shape, lambda b, s: (0,) * a.ndim)
    carry = pltpu.VMEM((8, width), f32)
    return pl.pallas_call(
        functools.partial(_rwkv_kernel, chunk=chunk),
        grid=(batch, seq // tm),
        in_specs=[col(0), col(1), col(2),
                  pl.BlockSpec((1, tm, SMALL_IN), lambda b, s: (b, s, 0)),
                  vec(0), vec(1), vec(2), full(mix_small),
                  vec(), full(w2p), vec(), full(a2p), full(g2p),
                  vec(), vec(), vec(), vec(), vec()],
        out_specs=pl.BlockSpec((1, tm, width), lambda b, s: (b, s, 0)),
        out_shape=jax.ShapeDtypeStruct((batch, seq, width), bf16),
        scratch_shapes=[pltpu.VMEM((RWKV_PAIRS, LANES, LANES), f32), carry, carry, carry,
                        pltpu.VMEM((8, SMALL_IN), f32)],
        compiler_params=_cparams(("parallel", "arbitrary")),
        name="rwkv7",
    )(rkv, rkv, rkv, small, mix_rkv, mix_rkv, mix_rkv, mix_small,
      w0, w2p, a0, a2p, g2p, k_k, k_a, r_k, ln_g, ln_b)


def _out_proj_kernel(x_ref, a_ref, b_ref, wa_ref, wb_ref, g_ref, o_ref, h_ref):
    acc = jnp.dot(a_ref[...], wa_ref[...], preferred_element_type=f32)
    acc = acc + jnp.dot(b_ref[...], wb_ref[...], preferred_element_type=f32)
    x2 = x_ref[...] + acc
    o_ref[...] = x2
    h_ref[...] = _rms(x2, g_ref[...]).astype(bf16)


def _out_proj(x, a, b, wa, wb, g, *, tm):
    m, n = x.shape
    row_block = lambda width: pl.BlockSpec((tm, width), lambda i: (i, 0))
    full = lambda arr: pl.BlockSpec(arr.shape, lambda i: (0, 0))
    return pl.pallas_call(
        _out_proj_kernel,
        grid=(m // tm,),
        in_specs=[row_block(n), row_block(a.shape[1]), row_block(b.shape[1]), full(wa), full(wb), full(g)],
        out_specs=[row_block(n), row_block(n)],
        out_shape=[jax.ShapeDtypeStruct((m, n), f32), jax.ShapeDtypeStruct((m, n), bf16)],
        compiler_params=_cparams(("parallel",)),
        name="out_proj",
    )(x, a, b, wa, wb, g)


def _ffn_kernel(x_ref, h_ref, wg_ref, wu_ref, wd_ref, o_ref, acc_ref):
    j = pl.program_id(1)

    @pl.when(j == 0)
    def _():
        acc_ref[...] = jnp.zeros_like(acc_ref)

    h = h_ref[...]
    gate = jnp.dot(h, wg_ref[...], preferred_element_type=f32)
    up = jnp.dot(h, wu_ref[...], preferred_element_type=f32)
    act = (gate / (1.0 + jnp.exp(-gate))) * up
    acc_ref[...] += jnp.dot(act.astype(bf16), wd_ref[...], preferred_element_type=f32)

    @pl.when(j == pl.num_programs(1) - 1)
    def _():
        o_ref[...] = x_ref[...] + acc_ref[...]


def _ffn(x, h, wg, wu, wd, *, tm, tf):
    m, d = x.shape
    hidden = wg.shape[1]
    assert m % tm == 0 and hidden % tf == 0
    return pl.pallas_call(
        _ffn_kernel,
        grid=(m // tm, hidden // tf),
        in_specs=[pl.BlockSpec((tm, d), lambda i, j: (i, 0)),
                  pl.BlockSpec((tm, d), lambda i, j: (i, 0)),
                  pl.BlockSpec((d, tf), lambda i, j: (0, j)),
                  pl.BlockSpec((d, tf), lambda i, j: (0, j)),
                  pl.BlockSpec((tf, d), lambda i, j: (j, 0))],
        out_specs=pl.BlockSpec((tm, d), lambda i, j: (i, 0)),
        out_shape=jax.ShapeDtypeStruct((m, d), f32),
        scratch_shapes=[pltpu.VMEM((tm, d), f32)],
        compiler_params=_cparams(("parallel", "arbitrary")),
        name="ffn",
    )(x, h, wg, wu, wd)


def _pad_cols(a, width):
    return jnp.pad(a, ((0, 0), (0, width - a.shape[1])))


def _pad_rows(a, rows, before=0):
    return jnp.pad(a, ((before, rows - before - a.shape[0]), (0, 0)))


def _rope_tables(seq):
    inv_freq = 1.0 / (ROPE_THETA ** (jnp.arange(0, MLA_ROPE, 2, dtype=f32) / MLA_ROPE))
    ang = jnp.arange(seq, dtype=f32)[:, None] * inv_freq[None, :]
    cos, sin = jnp.cos(ang), jnp.sin(ang)
    cos_t = jnp.tile(jnp.concatenate([cos, cos], axis=1), (1, MLA_HEADS))
    sin_t = jnp.tile(jnp.concatenate([-sin, sin], axis=1), (1, MLA_HEADS))
    return cos_t, sin_t


def _layer(x, attn_norm_g, w_in, q_lat_norm, w_uq, kv_lat_norm, w_ukv, q_head_norm, k_nope_norm, k_rope_norm,
           shift_mix, w0, w2, a0, a2, g2, k_k, k_a, r_k, ln_g, ln_b, w_out, ffn_norm_g, w_gate, w_up, w_down,
           cos_t, sin_t, *, tiles):
    batch, seq, d_model = x.shape
    tokens = batch * seq
    row = lambda a: a.reshape(1, -1)
    x2d = x.reshape(tokens, d_model)

    c_kr = MLA_Q_RANK + MLA_KV_RANK + MLA_ROPE
    c_v = c_kr + 3 * RWKV_WIDTH
    w_lat = jnp.concatenate([_pad_cols(w_in[:, :c_kr], MLA_IN), _pad_cols(w_in[:, c_v:], SMALL_IN)], axis=1).astype(bf16)
    w_rkv = w_in[:, c_kr:c_v].astype(bf16)
    mix_rkv = row(shift_mix[:3 * RWKV_WIDTH])
    mix_small = _pad_cols(row(shift_mix[3 * RWKV_WIDTH:]), SMALL_IN)

    h, lat, small = _norm_matmul(x2d, row(attn_norm_g), w_lat, tm=tiles["in_tm"],
                                 splits=(MLA_IN, SMALL_IN), out_dtypes=(bf16, f32))
    rkv = _matmul(h, w_rkv, tm=tiles["rkv_tm"], tn=tiles["in_tn"], out_dtype=bf16)

    wuq = w_uq.reshape(MLA_Q_RANK, MLA_HEADS, MLA_QK)
    wuq = jnp.concatenate([wuq[:, :, :MLA_NOPE].reshape(MLA_Q_RANK, -1),
                           wuq[:, :, MLA_NOPE:].reshape(MLA_Q_RANK, -1)], axis=1).astype(bf16)
    wukv = w_ukv.reshape(MLA_KV_RANK, MLA_HEADS, MLA_NOPE + MLA_V)
    wukv = jnp.concatenate([wukv[:, :, :MLA_NOPE].reshape(MLA_KV_RANK, -1),
                            wukv[:, :, MLA_NOPE:].reshape(MLA_KV_RANK, -1)], axis=1).astype(bf16)
    gqn = row(q_head_norm[:MLA_NOPE])
    gqp = row(jnp.tile(q_head_norm[MLA_NOPE:], MLA_HEADS))
    gkr = _pad_cols(row(k_rope_norm), LANES)
    q, k, vt = _mla_prep(lat, row(q_lat_norm), wuq, row(kv_lat_norm), wukv, gqn, gqp, row(k_nope_norm), gkr,
                         cos_t, sin_t, batch=batch, seq=seq, tm=tiles["tk"])
    out_a = _flash_attention(q, k, vt, tq=tiles["tq"], tk=tiles["tk"])

    w2p = _pad_rows(w2, LANES).astype(bf16)
    a2p = _pad_rows(a2, LANES, before=DECAY_LORA).astype(bf16)
    g2p = _pad_rows(g2, 2 * LANES).astype(bf16)
    out_b = _rwkv7(rkv.reshape(batch, seq, 3 * RWKV_WIDTH), small.reshape(batch, seq, SMALL_IN),
                   mix_rkv, mix_small, row(w0), w2p, row(a0), a2p, g2p, row(k_k), row(k_a), row(r_k),
                   row(ln_g), row(ln_b), tm=tiles["rw_tm"])

    mla_width = MLA_HEADS * MLA_V
    x2, h2 = _out_proj(x2d, out_a.reshape(tokens, mla_width), out_b.reshape(tokens, RWKV_WIDTH),
                       w_out[:mla_width].astype(bf16), w_out[mla_width:].astype(bf16), row(ffn_norm_g),
                       tm=tiles["out_tm"])
    y = _ffn(x2, h2, w_gate.astype(bf16), w_up.astype(bf16), w_down.astype(bf16),
             tm=tiles["ffn_tm"], tf=tiles["ffn_tf"])
    return y.reshape(batch, seq, d_model)


def _tiles(seq):
    t = lambda want: math.gcd(want, seq)
    return dict(in_tm=t(512), rkv_tm=t(1024), in_tn=1024, tq=t(1024), tk=t(1024) // 2, rw_tm=t(128),
                out_tm=t(512), ffn_tm=t(512), ffn_tf=512)


def kernel(x, attn_norm_g, w_in, q_lat_norm, w_uq, kv_lat_norm, w_ukv, q_head_norm, k_nope_norm, k_rope_norm, rwkv_shift_mix, rwkv_w0, rwkv_w2, rwkv_a0, rwkv_a2, rwkv_g2, rwkv_k_k, rwkv_k_a, rwkv_r_k, rwkv_ln_g, rwkv_ln_b, w_out, ffn_norm_g, w_gate, w_up, w_down):
    seq = x.shape[1]
    cos_t, sin_t = _rope_tables(seq)
    tiles = _tiles(seq)
    params = (attn_norm_g, w_in, q_lat_norm, w_uq, kv_lat_norm, w_ukv, q_head_norm, k_nope_norm, k_rope_norm,
              rwkv_shift_mix, rwkv_w0, rwkv_w2, rwkv_a0, rwkv_a2, rwkv_g2, rwkv_k_k, rwkv_k_a,
              rwkv_r_k.reshape(rwkv_r_k.shape[0], -1), rwkv_ln_g, rwkv_ln_b, w_out, ffn_norm_g, w_gate, w_up, w_down)
    for l in range(attn_norm_g.shape[0]):
        x = _layer(x, *(p[l] for p in params), cos_t, sin_t, tiles=tiles)
    return x
```

```python
import functools
import math

import jax
import jax.numpy as jnp
from jax import lax
from jax.experimental import pallas as pl
from jax.experimental.pallas import tpu as pltpu

LANES = 128
HALF = LANES // 2

MLA_HEADS = 8
MLA_Q_RANK = 768
MLA_KV_RANK = 512
MLA_NOPE = 128
MLA_ROPE = 64
MLA_QK = MLA_NOPE + MLA_ROPE
MLA_V = 128
ROPE_THETA = 10000.0
RWKV_HEAD = 64
RWKV_WIDTH = 1024
RWKV_PAIRS = RWKV_WIDTH // LANES
DECAY_LORA = 64
AAA_LORA = 64
GATE_LORA = 160
NORM_EPS = 1e-6
GN_EPS = 64e-5

MLA_IN = MLA_Q_RANK + MLA_KV_RANK + LANES
SMALL_IN = 3 * LANES
CHUNK = 64

VMEM_LIMIT = 56 * 1024 * 1024

f32 = jnp.float32
bf16 = jnp.bfloat16


def _cparams(sem):
    return pltpu.CompilerParams(dimension_semantics=sem, vmem_limit_bytes=VMEM_LIMIT)


def _rms(x, g):
    return x * lax.rsqrt(jnp.mean(x * x, axis=-1, keepdims=True) + NORM_EPS) * g


def _dot(a, b):
    return jnp.dot(a.astype(bf16), b.astype(bf16), preferred_element_type=f32)


def _dot_nt(a, b):
    return lax.dot_general(a.astype(bf16), b.astype(bf16), (((1,), (1,)), ((), ())),
                           preferred_element_type=f32)


def _norm_matmul_kernel(x_ref, g_ref, w_ref, h_ref, *o_refs, splits):
    h = _rms(x_ref[...], g_ref[...]).astype(bf16)
    h_ref[...] = h
    acc = jnp.dot(h, w_ref[...], preferred_element_type=f32)
    off = 0
    for o_ref, width in zip(o_refs, splits):
        o_ref[...] = acc[:, off:off + width].astype(o_ref.dtype)
        off += width


def _norm_matmul(x, g, w, *, tm, splits, out_dtypes):
    m, k = x.shape
    n = w.shape[1]
    assert m % tm == 0 and sum(splits) == n
    return pl.pallas_call(
        functools.partial(_norm_matmul_kernel, splits=splits),
        grid=(m // tm,),
        in_specs=[pl.BlockSpec((tm, k), lambda i: (i, 0)),
                  pl.BlockSpec((1, k), lambda i: (0, 0)),
                  pl.BlockSpec((k, n), lambda i: (0, 0))],
        out_specs=[pl.BlockSpec((tm, k), lambda i: (i, 0))] + [pl.BlockSpec((tm, s), lambda i: (i, 0)) for s in splits],
        out_shape=[jax.ShapeDtypeStruct((m, k), bf16)] + [jax.ShapeDtypeStruct((m, s), dt) for s, dt in zip(splits, out_dtypes)],
        compiler_params=_cparams(("parallel",)),
        name="norm_matmul",
    )(x, g, w)


def _matmul_kernel(h_ref, w_ref, o_ref):
    o_ref[...] = jnp.dot(h_ref[...], w_ref[...], preferred_element_type=f32).astype(o_ref.dtype)


def _matmul(h, w, *, tm, tn, out_dtype):
    m, k = h.shape
    n = w.shape[1]
    assert m % tm == 0 and n % tn == 0
    return pl.pallas_call(
        _matmul_kernel,
        grid=(m // tm, n // tn),
        in_specs=[pl.BlockSpec((tm, k), lambda i, j: (i, 0)),
                  pl.BlockSpec((k, tn), lambda i, j: (0, j))],
        out_specs=pl.BlockSpec((tm, tn), lambda i, j: (i, j)),
        out_shape=jax.ShapeDtypeStruct((m, n), out_dtype),
        compiler_params=_cparams(("parallel", "parallel")),
        name="matmul",
    )(h, w)


def _swap_halves(x, period):
    n = x.shape[-1]
    lane = lax.broadcasted_iota(jnp.int32, x.shape, x.ndim - 1)
    first = (lane % period) < (period // 2)
    return jnp.where(first, pltpu.roll(x, n - period // 2, x.ndim - 1), pltpu.roll(x, period // 2, x.ndim - 1))


def _lanes64(slab, hi):
    if hi:
        slab = pltpu.roll(slab, HALF, 1)
    return slab[:, :HALF]


def _mla_prep_kernel(lat_ref, gq_ref, wuq_ref, gkv_ref, wukv_ref, gqn_ref, gqp_ref, gkn_ref, gkr_ref,
                     cosq_ref, sinq_ref, q_ref, k_ref, v_ref):
    lat = lat_ref[...].astype(f32)
    scale = MLA_QK ** -0.5 * math.log2(math.e)
    rows = lat.shape[0]
    lane = lax.broadcasted_iota(jnp.int32, (rows, LANES), 1)
    lo = lane < HALF

    q = _dot(_rms(lat[:, :MLA_Q_RANK], gq_ref[...]), wuq_ref[...])
    pe = q[:, MLA_HEADS * MLA_NOPE:]
    pe_sq = pe * pe
    inv = []
    for h in range(MLA_HEADS):
        nope = q[:, h * MLA_NOPE:(h + 1) * MLA_NOPE]
        slab = pe_sq[:, (h // 2) * LANES:(h // 2 + 1) * LANES]
        ss = jnp.sum(nope * nope, axis=-1, keepdims=True)
        mine = lo if h % 2 == 0 else jnp.logical_not(lo)
        ss = ss + jnp.sum(jnp.where(mine, slab, 0.0), axis=-1, keepdims=True)
        inv.append(lax.rsqrt(ss * (1.0 / MLA_QK) + NORM_EPS))
    inv_pe = jnp.concatenate(
        [jnp.where(lo, inv[2 * p], inv[2 * p + 1]) for p in range(MLA_HEADS // 2)], axis=1)
    pe = pe * inv_pe * gqp_ref[...]
    pe = (pe * cosq_ref[...] + _swap_halves(pe, MLA_ROPE) * sinq_ref[...]) * scale
    for h in range(MLA_HEADS):
        nope = q[:, h * MLA_NOPE:(h + 1) * MLA_NOPE] * (inv[h] * scale) * gqn_ref[...]
        q_ref[0, h, :, :MLA_NOPE] = nope.astype(q_ref.dtype)
        slab = pe[:, (h // 2) * LANES:(h // 2 + 1) * LANES]
        q_ref[0, h, :, MLA_NOPE:] = _lanes64(slab, h % 2 == 1).astype(q_ref.dtype)

    kr = lat[:, MLA_Q_RANK + MLA_KV_RANK:]
    kr = kr * lax.rsqrt(jnp.sum(kr * kr, axis=-1, keepdims=True) * (1.0 / MLA_ROPE) + NORM_EPS) * gkr_ref[...]
    kr = kr * cosq_ref[:, :LANES] + _swap_halves(kr, MLA_ROPE) * sinq_ref[:, :LANES]
    kr = kr[:, :HALF].astype(k_ref.dtype)

    kv = _dot(_rms(lat[:, MLA_Q_RANK:MLA_Q_RANK + MLA_KV_RANK], gkv_ref[...]), wukv_ref[...])
    for h in range(MLA_HEADS):
        kn = kv[:, h * MLA_NOPE:(h + 1) * MLA_NOPE]
        k_ref[0, h, :, :MLA_NOPE] = _rms(kn, gkn_ref[...]).astype(k_ref.dtype)
        k_ref[0, h, :, MLA_NOPE:] = kr
        vh = kv[:, (MLA_HEADS + h) * MLA_V:(MLA_HEADS + h + 1) * MLA_V]
        v_ref[0, h, 0] = vh.T.astype(v_ref.dtype)


def _mla_prep(lat, gq, wuq, gkv, wukv, gqn, gqp, gkn, gkr, cosq, sinq, *, batch, seq, tm):
    nsb = seq // tm
    full = lambda a: pl.BlockSpec(a.shape, lambda b, s: (0,) * a.ndim)
    qk_spec = pl.BlockSpec((1, MLA_HEADS, tm, MLA_QK), lambda b, s: (b, 0, s, 0))
    return pl.pallas_call(
        _mla_prep_kernel,
        grid=(batch, nsb),
        in_specs=[pl.BlockSpec((tm, MLA_IN), lambda b, s: (b * nsb + s, 0)),
                  full(gq), full(wuq), full(gkv), full(wukv), full(gqn), full(gqp), full(gkn), full(gkr),
                  pl.BlockSpec((tm, cosq.shape[1]), lambda b, s: (s, 0)),
                  pl.BlockSpec((tm, sinq.shape[1]), lambda b, s: (s, 0))],
        out_specs=[qk_spec, qk_spec,
                   pl.BlockSpec((1, MLA_HEADS, 1, MLA_V, tm), lambda b, s: (b, 0, s, 0, 0))],
        out_shape=[jax.ShapeDtypeStruct((batch, MLA_HEADS, seq, MLA_QK), bf16),
                   jax.ShapeDtypeStruct((batch, MLA_HEADS, seq, MLA_QK), bf16),
                   jax.ShapeDtypeStruct((batch, MLA_HEADS, nsb, MLA_V, tm), bf16)],
        compiler_params=_cparams(("parallel", "parallel")),
        name="mla_prep",
    )(lat, gq, wuq, gkv, wukv, gqn, gqp, gkn, gkr, cosq, sinq)


def _flash_kernel(q_ref, k_ref, vt_ref, o_ref, *, tq, tk):
    assert tq == 2 * tk
    qi = pl.program_id(2)
    halves = (q_ref[0, 0, :tk, :], q_ref[0, 0, tk:, :])

    def scores(j, q):
        k = k_ref[0, 0, pl.ds(pl.multiple_of(j * tk, tk), tk), :]
        return lax.dot_general(k, q, (((1,), (1,)), ((), ())), preferred_element_type=f32)

    def update(j, s, carry, diagonal):
        m, l, acc = carry
        if diagonal:
            s = jnp.where(lax.broadcasted_iota(jnp.int32, s.shape, 0) <= lax.broadcasted_iota(jnp.int32, s.shape, 1),
                          s, -jnp.inf)
        m_new = jnp.maximum(m, jnp.max(s, axis=0, keepdims=True))
        p = jnp.exp2(s - m_new)
        alpha = jnp.exp2(m - m_new)
        l = alpha * l + jnp.sum(p, axis=0, keepdims=True)
        acc = alpha * acc + jnp.dot(vt_ref[0, 0, j], p.astype(bf16), preferred_element_type=f32)
        return m_new, l, acc

    def run(units, carry):
        carry = list(carry)
        ahead = 2
        s = [scores(j, halves[h]) for j, h, _ in units[:ahead]]
        for u, (j, h, diagonal) in enumerate(units):
            if u + ahead < len(units):
                s.append(scores(units[u + ahead][0], halves[units[u + ahead][1]]))
            carry[h] = update(j, s[u], carry[h], diagonal)
        return tuple(carry)

    def blocks(first_block, count):
        return lambda i, carry: run([(first_block + count * i + d, h, False) for d in range(count) for h in (0, 1)],
                                    carry)

    init = (jnp.full((1, tk), -jnp.inf, f32), jnp.zeros((1, tk), f32), jnp.zeros((MLA_V, tk), f32))
    carry = lax.fori_loop(0, qi // 2, blocks(0, 4), (init, init))
    carry = lax.fori_loop(0, qi % 2, blocks(4 * (qi // 2), 2), carry)
    first, second = run([(2 * qi, 0, True), (2 * qi, 1, False), (2 * qi + 1, 1, True)], carry)
    for half, (_, l, acc) in enumerate((first, second)):
        o_ref[0, half * tk:(half + 1) * tk, :] = (acc / l).T.astype(o_ref.dtype)


def _flash_attention(q, k, vt, *, tq, tk):
    batch, heads, seq, _ = q.shape
    assert seq % tq == 0 and tq == 2 * tk and vt.shape == (batch, heads, seq // tk, MLA_V, tk)
    return pl.pallas_call(
        functools.partial(_flash_kernel, tq=tq, tk=tk),
        grid=(batch, heads, seq // tq),
        in_specs=[pl.BlockSpec((1, 1, tq, MLA_QK), lambda b, h, i: (b, h, i, 0)),
                  pl.BlockSpec((1, 1, seq, MLA_QK), lambda b, h, i: (b, h, 0, 0)),
                  pl.BlockSpec((1, 1, seq // tk, MLA_V, tk), lambda b, h, i: (b, h, 0, 0, 0))],
        out_specs=pl.BlockSpec((1, tq, MLA_V), lambda b, h, i: (b, i, h)),
        out_shape=jax.ShapeDtypeStruct((batch, seq, heads * MLA_V), bf16),
        compiler_params=_cparams(("parallel", "parallel", "arbitrary")),
        name="flash_attention",
    )(q, k, vt)


def _slabs(x):
    return [x[:, p * LANES:(p + 1) * LANES] for p in range(x.shape[1] // LANES)]


def _pair_sum(x):
    lo = lax.broadcasted_iota(jnp.int32, (x.shape[0], LANES), 1) < HALF
    out = []
    for s in _slabs(x):
        s_lo = jnp.sum(jnp.where(lo, s, 0.0), axis=-1, keepdims=True)
        s_hi = jnp.sum(jnp.where(lo, 0.0, s), axis=-1, keepdims=True)
        out.append(jnp.where(lo, s_lo, s_hi))
    return jnp.concatenate(out, axis=1)


def _stack2(x, lo):
    return jnp.concatenate([jnp.where(lo, x, 0.0), jnp.where(lo, 0.0, x)], axis=0)


def _shift_rows(x, carry_ref):
    row = lax.broadcasted_iota(jnp.int32, x.shape, 0)
    prev = jnp.where(row == 0, carry_ref[7:8, :], pltpu.roll(x, 1, 0))
    carry_ref[...] = x[x.shape[0] - 8:, :]
    return prev


def _rwkv_kernel(r_ref, k_ref, v_ref, sm_ref, mixr_ref, mixk_ref, mixv_ref, mixs_ref,
                 w0_ref, w2_ref, a0_ref, a2_ref, g2_ref, kk_ref, ka_ref, rk_ref, lng_ref, lnb_ref,
                 o_ref, h_ref, cr_ref, ck_ref, cv_ref, cs_ref, *, chunk):
    @pl.when(pl.program_id(1) == 0)
    def _():
        h_ref[...] = jnp.zeros_like(h_ref)
        cr_ref[...] = jnp.zeros_like(cr_ref)
        ck_ref[...] = jnp.zeros_like(ck_ref)
        cv_ref[...] = jnp.zeros_like(cv_ref)
        cs_ref[...] = jnp.zeros_like(cs_ref)

    rows = r_ref.shape[1]
    L = chunk

    def mixed(x, carry_ref, mix_ref):
        return x + (_shift_rows(x, carry_ref) - x) * mix_ref[...]

    r = mixed(r_ref[0].astype(f32), cr_ref, mixr_ref)
    k = mixed(k_ref[0].astype(f32), ck_ref, mixk_ref)
    v = mixed(v_ref[0].astype(f32), cv_ref, mixv_ref)
    sm = mixed(sm_ref[0], cs_ref, mixs_ref)

    z = w0_ref[...] + _dot(jnp.tanh(sm[:, :LANES]), w2_ref[...])
    softplus = jnp.maximum(-z, 0.0) + jnp.log(1.0 + jnp.exp(-jnp.abs(z)))
    ld = -jnp.exp(-softplus - 0.5)
    a_sig = 1.0 / (1.0 + jnp.exp(-(a0_ref[...] + _dot(sm[:, :LANES], a2_ref[...]))))
    gate = _dot(1.0 / (1.0 + jnp.exp(-sm[:, LANES:])), g2_ref[...])

    kk = k * kk_ref[...]
    kk = kk * lax.rsqrt(jnp.maximum(_pair_sum(kk * kk), 1e-24))
    k = k * (1.0 + (a_sig - 1.0) * ka_ref[...])
    a_in = -kk
    b_in = kk * a_sig
    bonus = _pair_sum(r * k * rk_ref[...]) * v

    t_i = lax.broadcasted_iota(jnp.int32, (L, 2 * L), 0)
    s_i = lax.broadcasted_iota(jnp.int32, (L, 2 * L), 1) % L
    strict = s_i < t_i
    incl = s_i <= t_i
    tri = (lax.broadcasted_iota(jnp.int32, (L, L), 1) <= lax.broadcasted_iota(jnp.int32, (L, L), 0)).astype(bf16)
    rr = lax.broadcasted_iota(jnp.int32, (2 * L, 2 * L), 0)
    cc = lax.broadcasted_iota(jnp.int32, (2 * L, 2 * L), 1)
    bd = (rr < L) == (cc < L)
    eye = rr == cc
    eye_pair = jnp.where(eye[:L], 1.0, 0.0) + jnp.where(eye[L:], 1.0, 0.0)
    lo = lax.broadcasted_iota(jnp.int32, (L, LANES), 1) < HALF
    last_row = lax.broadcasted_iota(jnp.int32, (L, r.shape[1]), 0) == L - 1

    def blockdiag(pair):
        return jnp.where(bd, jnp.concatenate([pair, pair], axis=0), 0.0)

    n_chunks = rows // L
    n_pairs = r.shape[1] // LANES
    at, rt, bt, kt, bh, kh, vv, g_last = [], [], [], [], [], [], [], []
    for c in range(n_chunks):
        sl = slice(c * L, (c + 1) * L)
        ldc = ld[sl]
        d1 = ldc.astype(bf16)
        d2 = (ldc - d1.astype(f32)).astype(bf16)
        d3 = (ldc - d1.astype(f32) - d2.astype(f32)).astype(bf16)
        cs = (jnp.dot(tri, d1, preferred_element_type=f32) + jnp.dot(tri, d2, preferred_element_type=f32)
              + jnp.dot(tri, d3, preferred_element_type=f32))
        last = jnp.sum(jnp.where(last_row, cs, 0.0), axis=0, keepdims=True)
        g_inv = jnp.exp(-cs)
        g_rel = jnp.exp(last - cs)
        at += _slabs(a_in[sl] * jnp.exp(cs - ldc))
        rt += _slabs(r[sl] * jnp.exp(cs))
        bt += _slabs(b_in[sl] * g_inv)
        kt += _slabs(k[sl] * g_inv)
        bh += _slabs(b_in[sl] * g_rel)
        kh += _slabs(k[sl] * g_rel)
        vv += _slabs(v[sl])
        g_last += _slabs(jnp.exp(last))
    nu = range(n_chunks * n_pairs)
    gram = [_dot_nt(jnp.concatenate([at[u], rt[u]], axis=0),
                    jnp.concatenate([_stack2(bt[u], lo), _stack2(kt[u], lo)], axis=0)) for u in nu]
    a_ab = [jnp.where(strict, gram[u][:L, :2 * L], 0.0) for u in nu]
    a_ak = [jnp.where(strict, gram[u][:L, 2 * L:], 0.0) for u in nu]
    a_rb = [jnp.where(incl, gram[u][L:, :2 * L], 0.0) for u in nu]
    a_rk = [jnp.where(incl, gram[u][L:, 2 * L:], 0.0) for u in nu]
    v2 = [_stack2(vv[u], lo) for u in nu]
    av = [_dot(a_ak[u], v2[u]) for u in nu]
    n = a_ab
    t = [eye_pair + n[u] for u in nu]
    span = 1
    while 2 * span < L:
        n = [_dot(n[u], blockdiag(n[u])) for u in nu]
        t = [t[u] + _dot(t[u], blockdiag(n[u])) for u in nu]
        span *= 2
    pq = [_dot(t[u], jnp.concatenate([_stack2(at[u], lo), _stack2(av[u], lo)], axis=1)) for u in nu]
    zeros2 = jnp.zeros((2 * L, LANES), f32)
    zeros1 = jnp.zeros((L, LANES), f32)
    yy, mn = [], []
    for u in nu:
        p, q = pq[u][:, :LANES], pq[u][:, LANES:]
        yy.append(_dot(jnp.concatenate([a_rb[u], a_rk[u]], axis=1),
                       jnp.concatenate([jnp.concatenate([_stack2(p, lo), _stack2(q, lo)], axis=1),
                                        jnp.concatenate([zeros2, v2[u]], axis=1)], axis=0)))
        bk_t = jnp.concatenate([bh[u], kh[u]], axis=0).T
        mn.append(_dot(bk_t, jnp.concatenate([pq[u], jnp.concatenate([zeros1, vv[u]], axis=1)], axis=0)))
    hs = [h_ref[p] for p in range(n_pairs)]
    y_chunks = []
    for c in range(n_chunks):
        ys = []
        for p in range(n_pairs):
            u = c * n_pairs + p
            y1 = rt[u] + yy[u][:, :LANES]
            m = jnp.where(eye, g_last[u], 0.0) + jnp.where(bd, mn[u][:, :LANES], 0.0)
            yh = _dot(jnp.concatenate([y1, m], axis=0), hs[p])
            ys.append(yh[:L] + yy[u][:, LANES:])
            hs[p] = yh[L:] + jnp.where(bd, mn[u][:, LANES:], 0.0)
        y_chunks.append(jnp.concatenate(ys, axis=1))
    for p in range(n_pairs):
        h_ref[p] = hs[p]
    y = jnp.concatenate(y_chunks, axis=0)

    mu = _pair_sum(y) * (1.0 / RWKV_HEAD)
    yc = y - mu
    var = _pair_sum(yc * yc) * (1.0 / RWKV_HEAD)
    yn = yc * lax.rsqrt(var + GN_EPS) * lng_ref[...] + lnb_ref[...]
    o_ref[0] = ((yn + bonus) * gate).astype(o_ref.dtype)


def _rwkv7(rkv, small, mix_rkv, mix_small, w0, w2p, a0, a2p, g2p, k_k, k_a, r_k, ln_g, ln_b, *, tm, chunk=CHUNK):
    batch, seq, _ = rkv.shape
    assert seq % tm == 0 and tm % chunk == 0
    width = RWKV_WIDTH
    col = lambda j: pl.BlockSpec((1, tm, width), lambda b, s: (b, s, j))
    vec = lambda j=0: pl.BlockSpec((1, width), lambda b, s: (0, j))
    full = lambda a: pl.BlockSpec(a.shape, lambda b, s: (0,) * a.ndim)
    carry = pltpu.VMEM((8, width), f32)
    return pl.pallas_call(
        functools.partial(_rwkv_kernel, chunk=chunk),
        grid=(batch, seq // tm),
        in_specs=[col(0), col(1), col(2),
                  pl.BlockSpec((1, tm, SMALL_IN), lambda b, s: (b, s, 0)),
                  vec(0), vec(1), vec(2), full(mix_small),
                  vec(), full(w2p), vec(), full(a2p), full(g2p),
                  vec(), vec(), vec(), vec(), vec()],
        out_specs=pl.BlockSpec((1, tm, width), lambda b, s: (b, s, 0)),
        out_shape=jax.ShapeDtypeStruct((batch, seq, width), bf16),
        scratch_shapes=[pltpu.VMEM((RWKV_PAIRS, LANES, LANES), f32), carry, carry, carry,
                        pltpu.VMEM((8, SMALL_IN), f32)],
        compiler_params=_cparams(("parallel", "arbitrary")),
        name="rwkv7",
    )(rkv, rkv, rkv, small, mix_rkv, mix_rkv, mix_rkv, mix_small,
      w0, w2p, a0, a2p, g2p, k_k, k_a, r_k, ln_g, ln_b)


def _out_proj_kernel(x_ref, a_ref, b_ref, wa_ref, wb_ref, o_ref):
    acc = jnp.dot(a_ref[...], wa_ref[...], preferred_element_type=f32)
    acc = acc + jnp.dot(b_ref[...], wb_ref[...], preferred_element_type=f32)
    o_ref[...] = x_ref[...] + acc


def _out_proj(x, a, b, wa, wb, *, tm):
    m, n = x.shape
    row_block = lambda width: pl.BlockSpec((tm, width), lambda i: (i, 0))
    full = lambda arr: pl.BlockSpec(arr.shape, lambda i: (0, 0))
    return pl.pallas_call(
        _out_proj_kernel,
        grid=(m // tm,),
        in_specs=[row_block(n), row_block(a.shape[1]), row_block(b.shape[1]), full(wa), full(wb)],
        out_specs=row_block(n),
        out_shape=jax.ShapeDtypeStruct((m, n), f32),
        compiler_params=_cparams(("parallel",)),
        name="out_proj",
    )(x, a, b, wa, wb)


def _ffn_kernel(x_ref, g_ref, wg_ref, wu_ref, wd_ref, o_ref, h_ref, acc_ref):
    j = pl.program_id(1)

    @pl.when(j == 0)
    def _():
        h_ref[...] = _rms(x_ref[...], g_ref[...]).astype(bf16)
        acc_ref[...] = jnp.zeros_like(acc_ref)

    h = h_ref[...]
    gate = jnp.dot(h, wg_ref[...], preferred_element_type=f32)
    up = jnp.dot(h, wu_ref[...], preferred_element_type=f32)
    act = (gate / (1.0 + jnp.exp(-gate))) * up
    acc_ref[...] += jnp.dot(act.astype(bf16), wd_ref[...], preferred_element_type=f32)

    @pl.when(j == pl.num_programs(1) - 1)
    def _():
        o_ref[...] = x_ref[...] + acc_ref[...]


def _ffn(x, g, wg, wu, wd, *, tm, tf):
    m, d = x.shape
    hidden = wg.shape[1]
    assert m % tm == 0 and hidden % tf == 0
    return pl.pallas_call(
        _ffn_kernel,
        grid=(m // tm, hidden // tf),
        in_specs=[pl.BlockSpec((tm, d), lambda i, j: (i, 0)),
                  pl.BlockSpec((1, d), lambda i, j: (0, 0)),
                  pl.BlockSpec((d, tf), lambda i, j: (0, j)),
                  pl.BlockSpec((d, tf), lambda i, j: (0, j)),
                  pl.BlockSpec((tf, d), lambda i, j: (j, 0))],
        out_specs=pl.BlockSpec((tm, d), lambda i, j: (i, 0)),
        out_shape=jax.ShapeDtypeStruct((m, d), f32),
        scratch_shapes=[pltpu.VMEM((tm, d), bf16), pltpu.VMEM((tm, d), f32)],
        compiler_params=_cparams(("parallel", "arbitrary")),
        name="ffn",
    )(x, g, wg, wu, wd)


def _pad_cols(a, width):
    return jnp.pad(a, ((0, 0), (0, width - a.shape[1])))


def _pad_rows(a, rows, before=0):
    return jnp.pad(a, ((before, rows - before - a.shape[0]), (0, 0)))


def _rope_tables(seq):
    inv_freq = 1.0 / (ROPE_THETA ** (jnp.arange(0, MLA_ROPE, 2, dtype=f32) / MLA_ROPE))
    ang = jnp.arange(seq, dtype=f32)[:, None] * inv_freq[None, :]
    cos, sin = jnp.cos(ang), jnp.sin(ang)
    cos_t = jnp.tile(jnp.concatenate([cos, cos], axis=1), (1, MLA_HEADS))
    sin_t = jnp.tile(jnp.concatenate([-sin, sin], axis=1), (1, MLA_HEADS))
    return cos_t, sin_t


def _layer(x, attn_norm_g, w_in, q_lat_norm, w_uq, kv_lat_norm, w_ukv, q_head_norm, k_nope_norm, k_rope_norm,
           shift_mix, w0, w2, a0, a2, g2, k_k, k_a, r_k, ln_g, ln_b, w_out, ffn_norm_g, w_gate, w_up, w_down,
           cos_t, sin_t, *, tiles):
    batch, seq, d_model = x.shape
    tokens = batch * seq
    row = lambda a: a.reshape(1, -1)
    x2d = x.reshape(tokens, d_model)

    c_kr = MLA_Q_RANK + MLA_KV_RANK + MLA_ROPE
    c_v = c_kr + 3 * RWKV_WIDTH
    w_lat = jnp.concatenate([_pad_cols(w_in[:, :c_kr], MLA_IN), _pad_cols(w_in[:, c_v:], SMALL_IN)], axis=1).astype(bf16)
    w_rkv = w_in[:, c_kr:c_v].astype(bf16)
    mix_rkv = row(shift_mix[:3 * RWKV_WIDTH])
    mix_small = _pad_cols(row(shift_mix[3 * RWKV_WIDTH:]), SMALL_IN)

    h, lat, small = _norm_matmul(x2d, row(attn_norm_g), w_lat, tm=tiles["in_tm"],
                                 splits=(MLA_IN, SMALL_IN), out_dtypes=(bf16, f32))
    rkv = _matmul(h, w_rkv, tm=tiles["rkv_tm"], tn=tiles["in_tn"], out_dtype=bf16)

    wuq = w_uq.reshape(MLA_Q_RANK, MLA_HEADS, MLA_QK)
    wuq = jnp.concatenate([wuq[:, :, :MLA_NOPE].reshape(MLA_Q_RANK, -1),
                           wuq[:, :, MLA_NOPE:].reshape(MLA_Q_RANK, -1)], axis=1).astype(bf16)
    wukv = w_ukv.reshape(MLA_KV_RANK, MLA_HEADS, MLA_NOPE + MLA_V)
    wukv = jnp.concatenate([wukv[:, :, :MLA_NOPE].reshape(MLA_KV_RANK, -1),
                            wukv[:, :, MLA_NOPE:].reshape(MLA_KV_RANK, -1)], axis=1).astype(bf16)
    gqn = row(q_head_norm[:MLA_NOPE])
    gqp = row(jnp.tile(q_head_norm[MLA_NOPE:], MLA_HEADS))
    gkr = _pad_cols(row(k_rope_norm), LANES)
    q, k, vt = _mla_prep(lat, row(q_lat_norm), wuq, row(kv_lat_norm), wukv, gqn, gqp, row(k_nope_norm), gkr,
                         cos_t, sin_t, batch=batch, seq=seq, tm=tiles["tk"])
    out_a = _flash_attention(q, k, vt, tq=tiles["tq"], tk=tiles["tk"])

    w2p = _pad_rows(w2, LANES).astype(bf16)
    a2p = _pad_rows(a2, LANES, before=DECAY_LORA).astype(bf16)
    g2p = _pad_rows(g2, 2 * LANES).astype(bf16)
    out_b = _rwkv7(rkv.reshape(batch, seq, 3 * RWKV_WIDTH), small.reshape(batch, seq, SMALL_IN),
                   mix_rkv, mix_small, row(w0), w2p, row(a0), a2p, g2p, row(k_k), row(k_a), row(r_k),
                   row(ln_g), row(ln_b), tm=tiles["rw_tm"])

    mla_width = MLA_HEADS * MLA_V
    x2 = _out_proj(x2d, out_a.reshape(tokens, mla_width), out_b.reshape(tokens, RWKV_WIDTH),
                   w_out[:mla_width].astype(bf16), w_out[mla_width:].astype(bf16), tm=tiles["out_tm"])
    y = _ffn(x2, row(ffn_norm_g), w_gate.astype(bf16), w_up.astype(bf16), w_down.astype(bf16),
             tm=tiles["ffn_tm"], tf=tiles["ffn_tf"])
    return y.reshape(batch, seq, d_model)


def _tiles(seq):
    t = lambda want: math.gcd(want, seq)
    return dict(in_tm=t(512), rkv_tm=t(1024), in_tn=1024, tq=t(1024), tk=t(1024) // 2, rw_tm=t(128),
                out_tm=t(512), ffn_tm=t(512), ffn_tf=512)


def kernel(x, attn_norm_g, w_in, q_lat_norm, w_uq, kv_lat_norm, w_ukv, q_head_norm, k_nope_norm, k_rope_norm, rwkv_shift_mix, rwkv_w0, rwkv_w2, rwkv_a0, rwkv_a2, rwkv_g2, rwkv_k_k, rwkv_k_a, rwkv_r_k, rwkv_ln_g, rwkv_ln_b, w_out, ffn_norm_g, w_gate, w_up, w_down):
    seq = x.shape[1]
    cos_t, sin_t = _rope_tables(seq)
    tiles = _tiles(seq)
    params = (attn_norm_g, w_in, q_lat_norm, w_uq, kv_lat_norm, w_ukv, q_head_norm, k_nope_norm, k_rope_norm,
              rwkv_shift_mix, rwkv_w0, rwkv_w2, rwkv_a0, rwkv_a2, rwkv_g2, rwkv_k_k, rwkv_k_a,
              rwkv_r_k.reshape(rwkv_r_k.shape[0], -1), rwkv_ln_g, rwkv_ln_b, w_out, ffn_norm_g, w_gate, w_up, w_down)
    for l in range(attn_norm_g.shape[0]):
        x = _layer(x, *(p[l] for p in params), cos_t, sin_t, tiles=tiles)
    return x
```

```python
import functools
import math

import jax
import jax.numpy as jnp
from jax import lax
from jax.experimental import pallas as pl
from jax.experimental.pallas import tpu as pltpu

LANES = 128
HALF = LANES // 2

MLA_HEADS = 8
MLA_Q_RANK = 768
MLA_KV_RANK = 512
MLA_NOPE = 128
MLA_ROPE = 64
MLA_QK = MLA_NOPE + MLA_ROPE
MLA_V = 128
ROPE_THETA = 10000.0
RWKV_HEAD = 64
RWKV_WIDTH = 1024
RWKV_PAIRS = RWKV_WIDTH // LANES
DECAY_LORA = 64
AAA_LORA = 64
GATE_LORA = 160
NORM_EPS = 1e-6
GN_EPS = 64e-5

MLA_IN = MLA_Q_RANK + MLA_KV_RANK + LANES
SMALL_IN = 3 * LANES
CHUNK = 64

VMEM_LIMIT = 56 * 1024 * 1024

f32 = jnp.float32
bf16 = jnp.bfloat16


def _cparams(sem):
    return pltpu.CompilerParams(dimension_semantics=sem, vmem_limit_bytes=VMEM_LIMIT)


def _rms(x, g):
    return x * lax.rsqrt(jnp.mean(x * x, axis=-1, keepdims=True) + NORM_EPS) * g


def _dot(a, b):
    return jnp.dot(a.astype(bf16), b.astype(bf16), preferred_element_type=f32)


def _dot_nt(a, b):
    return lax.dot_general(a.astype(bf16), b.astype(bf16), (((1,), (1,)), ((), ())),
                           preferred_element_type=f32)


def _norm_matmul_kernel(x_ref, g_ref, w_ref, h_ref, *o_refs, splits):
    h = _rms(x_ref[...], g_ref[...]).astype(bf16)
    h_ref[...] = h
    acc = jnp.dot(h, w_ref[...], preferred_element_type=f32)
    off = 0
    for o_ref, width in zip(o_refs, splits):
        o_ref[...] = acc[:, off:off + width].astype(o_ref.dtype)
        off += width


def _norm_matmul(x, g, w, *, tm, splits, out_dtypes):
    m, k = x.shape
    n = w.shape[1]
    assert m % tm == 0 and sum(splits) == n
    return pl.pallas_call(
        functools.partial(_norm_matmul_kernel, splits=splits),
        grid=(m // tm,),
        in_specs=[pl.BlockSpec((tm, k), lambda i: (i, 0)),
                  pl.BlockSpec((1, k), lambda i: (0, 0)),
                  pl.BlockSpec((k, n), lambda i: (0, 0))],
        out_specs=[pl.BlockSpec((tm, k), lambda i: (i, 0))] + [pl.BlockSpec((tm, s), lambda i: (i, 0)) for s in splits],
        out_shape=[jax.ShapeDtypeStruct((m, k), bf16)] + [jax.ShapeDtypeStruct((m, s), dt) for s, dt in zip(splits, out_dtypes)],
        compiler_params=_cparams(("parallel",)),
        name="norm_matmul",
    )(x, g, w)


def _matmul_kernel(h_ref, w_ref, o_ref):
    o_ref[...] = jnp.dot(h_ref[...], w_ref[...], preferred_element_type=f32).astype(o_ref.dtype)


def _matmul(h, w, *, tm, tn, out_dtype):
    m, k = h.shape
    n = w.shape[1]
    assert m % tm == 0 and n % tn == 0
    return pl.pallas_call(
        _matmul_kernel,
        grid=(m // tm, n // tn),
        in_specs=[pl.BlockSpec((tm, k), lambda i, j: (i, 0)),
                  pl.BlockSpec((k, tn), lambda i, j: (0, j))],
        out_specs=pl.BlockSpec((tm, tn), lambda i, j: (i, j)),
        out_shape=jax.ShapeDtypeStruct((m, n), out_dtype),
        compiler_params=_cparams(("parallel", "parallel")),
        name="matmul",
    )(h, w)


def _swap_halves(x, period):
    n = x.shape[-1]
    lane = lax.broadcasted_iota(jnp.int32, x.shape, x.ndim - 1)
    first = (lane % period) < (period // 2)
    return jnp.where(first, pltpu.roll(x, n - period // 2, x.ndim - 1), pltpu.roll(x, period // 2, x.ndim - 1))


def _lanes64(slab, hi):
    if hi:
        slab = pltpu.roll(slab, HALF, 1)
    return slab[:, :HALF]


def _mla_prep_kernel(lat_ref, gq_ref, wuq_ref, gkv_ref, wukv_ref, gqn_ref, gqp_ref, gkn_ref, gkr_ref,
                     cosq_ref, sinq_ref, q_ref, k_ref, v_ref):
    lat = lat_ref[...].astype(f32)
    scale = MLA_QK ** -0.5 * math.log2(math.e)
    rows = lat.shape[0]
    lane = lax.broadcasted_iota(jnp.int32, (rows, LANES), 1)
    lo = lane < HALF

    q = _dot(_rms(lat[:, :MLA_Q_RANK], gq_ref[...]), wuq_ref[...])
    pe = q[:, MLA_HEADS * MLA_NOPE:]
    pe_sq = pe * pe
    inv = []
    for h in range(MLA_HEADS):
        nope = q[:, h * MLA_NOPE:(h + 1) * MLA_NOPE]
        slab = pe_sq[:, (h // 2) * LANES:(h // 2 + 1) * LANES]
        ss = jnp.sum(nope * nope, axis=-1, keepdims=True)
        mine = lo if h % 2 == 0 else jnp.logical_not(lo)
        ss = ss + jnp.sum(jnp.where(mine, slab, 0.0), axis=-1, keepdims=True)
        inv.append(lax.rsqrt(ss * (1.0 / MLA_QK) + NORM_EPS))
    inv_pe = jnp.concatenate(
        [jnp.where(lo, inv[2 * p], inv[2 * p + 1]) for p in range(MLA_HEADS // 2)], axis=1)
    pe = pe * inv_pe * gqp_ref[...]
    pe = (pe * cosq_ref[...] + _swap_halves(pe, MLA_ROPE) * sinq_ref[...]) * scale
    for h in range(MLA_HEADS):
        nope = q[:, h * MLA_NOPE:(h + 1) * MLA_NOPE] * (inv[h] * scale) * gqn_ref[...]
        q_ref[0, h, :, :MLA_NOPE] = nope.astype(q_ref.dtype)
        slab = pe[:, (h // 2) * LANES:(h // 2 + 1) * LANES]
        q_ref[0, h, :, MLA_NOPE:] = _lanes64(slab, h % 2 == 1).astype(q_ref.dtype)

    kr = lat[:, MLA_Q_RANK + MLA_KV_RANK:]
    kr = kr * lax.rsqrt(jnp.sum(kr * kr, axis=-1, keepdims=True) * (1.0 / MLA_ROPE) + NORM_EPS) * gkr_ref[...]
    kr = kr * cosq_ref[:, :LANES] + _swap_halves(kr, MLA_ROPE) * sinq_ref[:, :LANES]
    kr = kr[:, :HALF].astype(k_ref.dtype)

    kv = _dot(_rms(lat[:, MLA_Q_RANK:MLA_Q_RANK + MLA_KV_RANK], gkv_ref[...]), wukv_ref[...])
    for h in range(MLA_HEADS):
        kn = kv[:, h * MLA_NOPE:(h + 1) * MLA_NOPE]
        k_ref[0, h, :, :MLA_NOPE] = _rms(kn, gkn_ref[...]).astype(k_ref.dtype)
        k_ref[0, h, :, MLA_NOPE:] = kr
        vh = kv[:, (MLA_HEADS + h) * MLA_V:(MLA_HEADS + h + 1) * MLA_V]
        v_ref[0, h, 0] = vh.T.astype(v_ref.dtype)


def _mla_prep(lat, gq, wuq, gkv, wukv, gqn, gqp, gkn, gkr, cosq, sinq, *, batch, seq, tm):
    nsb = seq // tm
    full = lambda a: pl.BlockSpec(a.shape, lambda b, s: (0,) * a.ndim)
    qk_spec = pl.BlockSpec((1, MLA_HEADS, tm, MLA_QK), lambda b, s: (b, 0, s, 0))
    return pl.pallas_call(
        _mla_prep_kernel,
        grid=(batch, nsb),
        in_specs=[pl.BlockSpec((tm, MLA_IN), lambda b, s: (b * nsb + s, 0)),
                  full(gq), full(wuq), full(gkv), full(wukv), full(gqn), full(gqp), full(gkn), full(gkr),
                  pl.BlockSpec((tm, cosq.shape[1]), lambda b, s: (s, 0)),
                  pl.BlockSpec((tm, sinq.shape[1]), lambda b, s: (s, 0))],
        out_specs=[qk_spec, qk_spec,
                   pl.BlockSpec((1, MLA_HEADS, 1, MLA_V, tm), lambda b, s: (b, 0, s, 0, 0))],
        out_shape=[jax.ShapeDtypeStruct((batch, MLA_HEADS, seq, MLA_QK), bf16),
                   jax.ShapeDtypeStruct((batch, MLA_HEADS, seq, MLA_QK), bf16),
                   jax.ShapeDtypeStruct((batch, MLA_HEADS, nsb, MLA_V, tm), bf16)],
        compiler_params=_cparams(("parallel", "parallel")),
        name="mla_prep",
    )(lat, gq, wuq, gkv, wukv, gqn, gqp, gkn, gkr, cosq, sinq)


def _flash_kernel(q_ref, k_ref, vt_ref, o_ref, *, tq, tk):
    assert tq == 2 * tk
    qi = pl.program_id(2)
    halves = (q_ref[0, 0, :tk, :], q_ref[0, 0, tk:, :])

    def scores(j, q):
        k = k_ref[0, 0, pl.ds(pl.multiple_of(j * tk, tk), tk), :]
        return lax.dot_general(k, q, (((1,), (1,)), ((), ())), preferred_element_type=f32)

    def update(j, s, carry, diagonal):
        m, l, acc = carry
        if diagonal:
            s = jnp.where(lax.broadcasted_iota(jnp.int32, s.shape, 0) <= lax.broadcasted_iota(jnp.int32, s.shape, 1),
                          s, -jnp.inf)
        m_new = jnp.maximum(m, jnp.max(s, axis=0, keepdims=True))
        p = jnp.exp2(s - m_new)
        alpha = jnp.exp2(m - m_new)
        l = alpha * l + jnp.sum(p, axis=0, keepdims=True)
        acc = alpha * acc + jnp.dot(vt_ref[0, 0, j], p.astype(bf16), preferred_element_type=f32)
        return m_new, l, acc

    def run(units, carry):
        carry = list(carry)
        ahead = 2
        s = [scores(j, halves[h]) for j, h, _ in units[:ahead]]
        for u, (j, h, diagonal) in enumerate(units):
            if u + ahead < len(units):
                s.append(scores(units[u + ahead][0], halves[units[u + ahead][1]]))
            carry[h] = update(j, s[u], carry[h], diagonal)
        return tuple(carry)

    def blocks(first_block, count):
        return lambda i, carry: run([(first_block + count * i + d, h, False) for d in range(count) for h in (0, 1)],
                                    carry)

    init = (jnp.full((1, tk), -jnp.inf, f32), jnp.zeros((1, tk), f32), jnp.zeros((MLA_V, tk), f32))
    carry = lax.fori_loop(0, qi // 2, blocks(0, 4), (init, init))
    carry = lax.fori_loop(0, qi % 2, blocks(4 * (qi // 2), 2), carry)
    first, second = run([(2 * qi, 0, True), (2 * qi, 1, False), (2 * qi + 1, 1, True)], carry)
    for half, (_, l, acc) in enumerate((first, second)):
        o_ref[0, half * tk:(half + 1) * tk, :] = (acc / l).T.astype(o_ref.dtype)


def _flash_attention(q, k, vt, *, tq, tk):
    batch, heads, seq, _ = q.shape
    assert seq % tq == 0 and tq == 2 * tk and vt.shape == (batch, heads, seq // tk, MLA_V, tk)
    return pl.pallas_call(
        functools.partial(_flash_kernel, tq=tq, tk=tk),
        grid=(batch, heads, seq // tq),
        in_specs=[pl.BlockSpec((1, 1, tq, MLA_QK), lambda b, h, i: (b, h, i, 0)),
                  pl.BlockSpec((1, 1, seq, MLA_QK), lambda b, h, i: (b, h, 0, 0)),
                  pl.BlockSpec((1, 1, seq // tk, MLA_V, tk), lambda b, h, i: (b, h, 0, 0, 0))],
        out_specs=pl.BlockSpec((1, tq, MLA_V), lambda b, h, i: (b, i, h)),
        out_shape=jax.ShapeDtypeStruct((batch, seq, heads * MLA_V), bf16),
        compiler_params=_cparams(("parallel", "parallel", "arbitrary")),
        name="flash_attention",
    )(q, k, vt)


def _slabs(x):
    return [x[:, p * LANES:(p + 1) * LANES] for p in range(x.shape[1] // LANES)]


def _pair_sum(x):
    lo = lax.broadcasted_iota(jnp.int32, (x.shape[0], LANES), 1) < HALF
    out = []
    for s in _slabs(x):
        s_lo = jnp.sum(jnp.where(lo, s, 0.0), axis=-1, keepdims=True)
        s_hi = jnp.sum(jnp.where(lo, 0.0, s), axis=-1, keepdims=True)
        out.append(jnp.where(lo, s_lo, s_hi))
    return jnp.concatenate(out, axis=1)


def _stack2(x, lo):
    return jnp.concatenate([jnp.where(lo, x, 0.0), jnp.where(lo, 0.0, x)], axis=0)


def _shift_rows(x, carry_ref):
    row = lax.broadcasted_iota(jnp.int32, x.shape, 0)
    prev = jnp.where(row == 0, carry_ref[7:8, :], pltpu.roll(x, 1, 0))
    carry_ref[...] = x[x.shape[0] - 8:, :]
    return prev


def _rwkv_kernel(r_ref, k_ref, v_ref, sm_ref, mixr_ref, mixk_ref, mixv_ref, mixs_ref,
                 w0_ref, w2_ref, a0_ref, a2_ref, g2_ref, kk_ref, ka_ref, rk_ref, lng_ref, lnb_ref,
                 o_ref, h_ref, cr_ref, ck_ref, cv_ref, cs_ref, *, chunk):
    @pl.when(pl.program_id(1) == 0)
    def _():
        h_ref[...] = jnp.zeros_like(h_ref)
        cr_ref[...] = jnp.zeros_like(cr_ref)
        ck_ref[...] = jnp.zeros_like(ck_ref)
        cv_ref[...] = jnp.zeros_like(cv_ref)
        cs_ref[...] = jnp.zeros_like(cs_ref)

    rows = r_ref.shape[1]
    L = chunk

    def mixed(x, carry_ref, mix_ref):
        return x + (_shift_rows(x, carry_ref) - x) * mix_ref[...]

    r = mixed(r_ref[0].astype(f32), cr_ref, mixr_ref)
    k = mixed(k_ref[0].astype(f32), ck_ref, mixk_ref)
    v = mixed(v_ref[0].astype(f32), cv_ref, mixv_ref)
    sm = mixed(sm_ref[0], cs_ref, mixs_ref)

    z = w0_ref[...] + _dot(jnp.tanh(sm[:, :LANES]), w2_ref[...])
    softplus = jnp.maximum(-z, 0.0) + jnp.log(1.0 + jnp.exp(-jnp.abs(z)))
    ld = -jnp.exp(-softplus - 0.5)
    a_sig = 1.0 / (1.0 + jnp.exp(-(a0_ref[...] + _dot(sm[:, :LANES], a2_ref[...]))))
    gate = _dot(1.0 / (1.0 + jnp.exp(-sm[:, LANES:])), g2_ref[...])

    kk = k * kk_ref[...]
    kk = kk * lax.rsqrt(jnp.maximum(_pair_sum(kk * kk), 1e-24))
    k = k * (1.0 + (a_sig - 1.0) * ka_ref[...])
    a_in = -kk
    b_in = kk * a_sig
    bonus = _pair_sum(r * k * rk_ref[...]) * v

    t_i = lax.broadcasted_iota(jnp.int32, (L, 2 * L), 0)
    s_i = lax.broadcasted_iota(jnp.int32, (L, 2 * L), 1) % L
    strict = s_i < t_i
    incl = s_i <= t_i
    tri = (lax.broadcasted_iota(jnp.int32, (L, L), 1) <= lax.broadcasted_iota(jnp.int32, (L, L), 0)).astype(bf16)
    rr = lax.broadcasted_iota(jnp.int32, (2 * L, 2 * L), 0)
    cc = lax.broadcasted_iota(jnp.int32, (2 * L, 2 * L), 1)
    bd = (rr < L) == (cc < L)
    eye = rr == cc
    eye_pair = jnp.where(eye[:L], 1.0, 0.0) + jnp.where(eye[L:], 1.0, 0.0)
    lo = lax.broadcasted_iota(jnp.int32, (L, LANES), 1) < HALF
    last_row = lax.broadcasted_iota(jnp.int32, (L, r.shape[1]), 0) == L - 1

    def blockdiag(pair):
        return jnp.where(bd, jnp.concatenate([pair, pair], axis=0), 0.0)

    n_chunks = rows // L
    n_pairs = r.shape[1] // LANES
    at, rt, bt, kt, bh, kh, vv, g_last = [], [], [], [], [], [], [], []
    for c in range(n_chunks):
        sl = slice(c * L, (c + 1) * L)
        ldc = ld[sl]
        d1 = ldc.astype(bf16)
        d2 = (ldc - d1.astype(f32)).astype(bf16)
        d3 = (ldc - d1.astype(f32) - d2.astype(f32)).astype(bf16)
        cs = (jnp.dot(tri, d1, preferred_element_type=f32) + jnp.dot(tri, d2, preferred_element_type=f32)
              + jnp.dot(tri, d3, preferred_element_type=f32))
        last = jnp.sum(jnp.where(last_row, cs, 0.0), axis=0, keepdims=True)
        g_inv = jnp.exp(-cs)
        g_rel = jnp.exp(last - cs)
        at += _slabs(a_in[sl] * jnp.exp(cs - ldc))
        rt += _slabs(r[sl] * jnp.exp(cs))
        bt += _slabs(b_in[sl] * g_inv)
        kt += _slabs(k[sl] * g_inv)
        bh += _slabs(b_in[sl] * g_rel)
        kh += _slabs(k[sl] * g_rel)
        vv += _slabs(v[sl])
        g_last += _slabs(jnp.exp(last))
    nu = range(n_chunks * n_pairs)
    gram = [_dot_nt(jnp.concatenate([at[u], rt[u]], axis=0),
                    jnp.concatenate([_stack2(bt[u], lo), _stack2(kt[u], lo)], axis=0)) for u in nu]
    a_ab = [jnp.where(strict, gram[u][:L, :2 * L], 0.0) for u in nu]
    a_ak = [jnp.where(strict, gram[u][:L, 2 * L:], 0.0) for u in nu]
    a_rb = [jnp.where(incl, gram[u][L:, :2 * L], 0.0) for u in nu]
    a_rk = [jnp.where(incl, gram[u][L:, 2 * L:], 0.0) for u in nu]
    v2 = [_stack2(vv[u], lo) for u in nu]
    av = [_dot(a_ak[u], v2[u]) for u in nu]
    n = a_ab
    t = [eye_pair + n[u] for u in nu]
    span = 1
    while 2 * span < L:
        n = [_dot(n[u], blockdiag(n[u])) for u in nu]
        t = [t[u] + _dot(t[u], blockdiag(n[u])) for u in nu]
        span *= 2
    pq = [_dot(t[u], jnp.concatenate([_stack2(at[u], lo), _stack2(av[u], lo)], axis=1)) for u in nu]
    zeros2 = jnp.zeros((2 * L, LANES), f32)
    zeros1 = jnp.zeros((L, LANES), f32)
    yy, mn = [], []
    for u in nu:
        p, q = pq[u][:, :LANES], pq[u][:, LANES:]
        yy.append(_dot(jnp.concatenate([a_rb[u], a_rk[u]], axis=1),
                       jnp.concatenate([jnp.concatenate([_stack2(p, lo), _stack2(q, lo)], axis=1),
                                        jnp.concatenate([zeros2, v2[u]], axis=1)], axis=0)))
        bk_t = jnp.concatenate([bh[u], kh[u]], axis=0).T
        mn.append(_dot(bk_t, jnp.concatenate([pq[u], jnp.concatenate([zeros1, vv[u]], axis=1)], axis=0)))
    hs = [h_ref[p] for p in range(n_pairs)]
    y_chunks = []
    for c in range(n_chunks):
        ys = []
        for p in range(n_pairs):
            u = c * n_pairs + p
            y1 = rt[u] + yy[u][:, :LANES]
            m = jnp.where(eye, g_last[u], 0.0) + jnp.where(bd, mn[u][:, :LANES], 0.0)
            yh = _dot(jnp.concatenate([y1, m], axis=0), hs[p])
            ys.append(yh[:L] + yy[u][:, LANES:])
            hs[p] = yh[L:] + jnp.where(bd, mn[u][:, LANES:], 0.0)
        y_chunks.append(jnp.concatenate(ys, axis=1))
    for p in range(n_pairs):
        h_ref[p] = hs[p]
    y = jnp.concatenate(y_chunks, axis=0)

    mu = _pair_sum(y) * (1.0 / RWKV_HEAD)
    yc = y - mu
    var = _pair_sum(yc * yc) * (1.0 / RWKV_HEAD)
    yn = yc * lax.rsqrt(var + GN_EPS) * lng_ref[...] + lnb_ref[...]
    o_ref[0] = ((yn + bonus) * gate).astype(o_ref.dtype)


def _rwkv7(rkv, small, mix_rkv, mix_small, w0, w2p, a0, a2p, g2p, k_k, k_a, r_k, ln_g, ln_b, *, tm, chunk=CHUNK):
    batch, seq, _ = rkv.shape
    assert seq % tm == 0 and tm % chunk == 0
    width = RWKV_WIDTH
    col = lambda j: pl.BlockSpec((1, tm, width), lambda b, s: (b, s, j))
    vec = lambda j=0: pl.BlockSpec((1, width), lambda b, s: (0, j))
    full = lambda a: pl.BlockSpec(a.shape, lambda b, s: (0,) * a.ndim)
    carry = pltpu.VMEM((8, width), f32)
    return pl.pallas_call(
        functools.partial(_rwkv_kernel, chunk=chunk),
        grid=(batch, seq // tm),
        in_specs=[col(0), col(1), col(2),
                  pl.BlockSpec((1, tm, SMALL_IN), lambda b, s: (b, s, 0)),
                  vec(0), vec(1), vec(2), full(mix_small),
                  vec(), full(w2p), vec(), full(a2p), full(g2p),
                  vec(), vec(), vec(), vec(), vec()],
        out_specs=pl.BlockSpec((1, tm, width), lambda b, s: (b, s, 0)),
        out_shape=jax.ShapeDtypeStruct((batch, seq, width), bf16),
        scratch_shapes=[pltpu.VMEM((RWKV_PAIRS, LANES, LANES), f32), carry, carry, carry,
                        pltpu.VMEM((8, SMALL_IN), f32)],
        compiler_params=_cparams(("parallel", "arbitrary")),
        name="rwkv7",
    )(rkv, rkv, rkv, small, mix_rkv, mix_rkv, mix_rkv, mix_small,
      w0, w2p, a0, a2p, g2p, k_k, k_a, r_k, ln_g, ln_b)


def _out_proj_kernel(x_ref, a_ref, b_ref, wa_ref, wb_ref, o_ref):
    acc = jnp.dot(a_ref[...], wa_ref[...], preferred_element_type=f32)
    acc = acc + jnp.dot(b_ref[...], wb_ref[...], preferred_element_type=f32)
    o_ref[...] = x_ref[...] + acc


def _out_proj(x, a, b, wa, wb, *, tm):
    m, n = x.shape
    row_block = lambda width: pl.BlockSpec((tm, width), lambda i: (i, 0))
    full = lambda arr: pl.BlockSpec(arr.shape, lambda i: (0, 0))
    return pl.pallas_call(
        _out_proj_kernel,
        grid=(m // tm,),
        in_specs=[row_block(n), row_block(a.shape[1]), row_block(b.shape[1]), full(wa), full(wb)],
        out_specs=row_block(n),
        out_shape=jax.ShapeDtypeStruct((m, n), f32),
        compiler_params=_cparams(("parallel",)),
        name="out_proj",
    )(x, a, b, wa, wb)


def _ffn_kernel(x_ref, g_ref, wg_ref, wu_ref, wd_ref, o_ref, h_ref):
    @pl.when(pl.program_id(1) == 0)
    def _():
        x = x_ref[...]
        h_ref[...] = _rms(x, g_ref[...]).astype(bf16)
        o_ref[...] = x

    h = h_ref[...]
    gate = jnp.dot(h, wg_ref[...], preferred_element_type=f32)
    up = jnp.dot(h, wu_ref[...], preferred_element_type=f32)
    act = (gate / (1.0 + jnp.exp(-gate))) * up
    o_ref[...] += jnp.dot(act.astype(bf16), wd_ref[...], preferred_element_type=f32)


def _ffn(x, g, wg, wu, wd, *, tm, tf):
    m, d = x.shape
    hidden = wg.shape[1]
    assert m % tm == 0 and hidden % tf == 0
    return pl.pallas_call(
        _ffn_kernel,
        grid=(m // tm, hidden // tf),
        in_specs=[pl.BlockSpec((tm, d), lambda i, j: (i, 0)),
                  pl.BlockSpec((1, d), lambda i, j: (0, 0)),
                  pl.BlockSpec((d, tf), lambda i, j: (0, j)),
                  pl.BlockSpec((d, tf), lambda i, j: (0, j)),
                  pl.BlockSpec((tf, d), lambda i, j: (j, 0))],
        out_specs=pl.BlockSpec((tm, d), lambda i, j: (i, 0)),
        out_shape=jax.ShapeDtypeStruct((m, d), f32),
        scratch_shapes=[pltpu.VMEM((tm, d), bf16)],
        compiler_params=_cparams(("parallel", "arbitrary")),
        name="ffn",
    )(x, g, wg, wu, wd)


def _pad_cols(a, width):
    return jnp.pad(a, ((0, 0), (0, width - a.shape[1])))


def _pad_rows(a, rows, before=0):
    return jnp.pad(a, ((before, rows - before - a.shape[0]), (0, 0)))


def _rope_tables(seq):
    inv_freq = 1.0 / (ROPE_THETA ** (jnp.arange(0, MLA_ROPE, 2, dtype=f32) / MLA_ROPE))
    ang = jnp.arange(seq, dtype=f32)[:, None] * inv_freq[None, :]
    cos, sin = jnp.cos(ang), jnp.sin(ang)
    cos_t = jnp.tile(jnp.concatenate([cos, cos], axis=1), (1, MLA_HEADS))
    sin_t = jnp.tile(jnp.concatenate([-sin, sin], axis=1), (1, MLA_HEADS))
    return cos_t, sin_t


def _layer(x, attn_norm_g, w_in, q_lat_norm, w_uq, kv_lat_norm, w_ukv, q_head_norm, k_nope_norm, k_rope_norm,
           shift_mix, w0, w2, a0, a2, g2, k_k, k_a, r_k, ln_g, ln_b, w_out, ffn_norm_g, w_gate, w_up, w_down,
           cos_t, sin_t, *, tiles):
    batch, seq, d_model = x.shape
    tokens = batch * seq
    row = lambda a: a.reshape(1, -1)
    x2d = x.reshape(tokens, d_model)

    c_kr = MLA_Q_RANK + MLA_KV_RANK + MLA_ROPE
    c_v = c_kr + 3 * RWKV_WIDTH
    w_lat = jnp.concatenate([_pad_cols(w_in[:, :c_kr], MLA_IN), _pad_cols(w_in[:, c_v:], SMALL_IN)], axis=1).astype(bf16)
    w_rkv = w_in[:, c_kr:c_v].astype(bf16)
    mix_rkv = row(shift_mix[:3 * RWKV_WIDTH])
    mix_small = _pad_cols(row(shift_mix[3 * RWKV_WIDTH:]), SMALL_IN)

    h, lat, small = _norm_matmul(x2d, row(attn_norm_g), w_lat, tm=tiles["in_tm"],
                                 splits=(MLA_IN, SMALL_IN), out_dtypes=(bf16, f32))
    rkv = _matmul(h, w_rkv, tm=tiles["rkv_tm"], tn=tiles["in_tn"], out_dtype=bf16)

    wuq = w_uq.reshape(MLA_Q_RANK, MLA_HEADS, MLA_QK)
    wuq = jnp.concatenate([wuq[:, :, :MLA_NOPE].reshape(MLA_Q_RANK, -1),
                           wuq[:, :, MLA_NOPE:].reshape(MLA_Q_RANK, -1)], axis=1).astype(bf16)
    wukv = w_ukv.reshape(MLA_KV_RANK, MLA_HEADS, MLA_NOPE + MLA_V)
    wukv = jnp.concatenate([wukv[:, :, :MLA_NOPE].reshape(MLA_KV_RANK, -1),
                            wukv[:, :, MLA_NOPE:].reshape(MLA_KV_RANK, -1)], axis=1).astype(bf16)
    gqn = row(q_head_norm[:MLA_NOPE])
    gqp = row(jnp.tile(q_head_norm[MLA_NOPE:], MLA_HEADS))
    gkr = _pad_cols(row(k_rope_norm), LANES)
    q, k, vt = _mla_prep(lat, row(q_lat_norm), wuq, row(kv_lat_norm), wukv, gqn, gqp, row(k_nope_norm), gkr,
                         cos_t, sin_t, batch=batch, seq=seq, tm=tiles["tk"])
    out_a = _flash_attention(q, k, vt, tq=tiles["tq"], tk=tiles["tk"])

    w2p = _pad_rows(w2, LANES).astype(bf16)
    a2p = _pad_rows(a2, LANES, before=DECAY_LORA).astype(bf16)
    g2p = _pad_rows(g2, 2 * LANES).astype(bf16)
    out_b = _rwkv7(rkv.reshape(batch, seq, 3 * RWKV_WIDTH), small.reshape(batch, seq, SMALL_IN),
                   mix_rkv, mix_small, row(w0), w2p, row(a0), a2p, g2p, row(k_k), row(k_a), row(r_k),
                   row(ln_g), row(ln_b), tm=tiles["rw_tm"])

    mla_width = MLA_HEADS * MLA_V
    x2 = _out_proj(x2d, out_a.reshape(tokens, mla_width), out_b.reshape(tokens, RWKV_WIDTH),
                   w_out[:mla_width].astype(bf16), w_out[mla_width:].astype(bf16), tm=tiles["out_tm"])
    y = _ffn(x2, row(ffn_norm_g), w_gate.astype(bf16), w_up.astype(bf16), w_down.astype(bf16),
             tm=tiles["ffn_tm"], tf=tiles["ffn_tf"])
    return y.reshape(batch, seq, d_model)


def _tiles(seq):
    t = lambda want: math.gcd(want, seq)
    return dict(in_tm=t(512), rkv_tm=t(1024), in_tn=1024, tq=t(1024), tk=t(1024) // 2, rw_tm=t(128),
                out_tm=t(512), ffn_tm=t(1024), ffn_tf=512)


def kernel(x, attn_norm_g, w_in, q_lat_norm, w_uq, kv_lat_norm, w_ukv, q_head_norm, k_nope_norm, k_rope_norm, rwkv_shift_mix, rwkv_w0, rwkv_w2, rwkv_a0, rwkv_a2, rwkv_g2, rwkv_k_k, rwkv_k_a, rwkv_r_k, rwkv_ln_g, rwkv_ln_b, w_out, ffn_norm_g, w_gate, w_up, w_down):
    seq = x.shape[1]
    cos_t, sin_t = _rope_tables(seq)
    tiles = _tiles(seq)
    params = (attn_norm_g, w_in, q_lat_norm, w_uq, kv_lat_norm, w_ukv, q_head_norm, k_nope_norm, k_rope_norm,
              rwkv_shift_mix, rwkv_w0, rwkv_w2, rwkv_a0, rwkv_a2, rwkv_g2, rwkv_k_k, rwkv_k_a,
              rwkv_r_k.reshape(rwkv_r_k.shape[0], -1), rwkv_ln_g, rwkv_ln_b, w_out, ffn_norm_g, w_gate, w_up, w_down)
    for l in range(attn_norm_g.shape[0]):
        x = _layer(x, *(p[l] for p in params), cos_t, sin_t, tiles=tiles)
    return x
```
